```python
import math
import jax
import jax.numpy as jnp
from jax import lax
import numpy as np

D_MODEL = 1024
BATCH = 16
SEQ = 4096
DEPTH = 2
DEC_BATCH = 16
DEC_SEQ = 32
PAST_LEN = 1024

CHUNK = 64
NORM_EPS = 1e-6
RWKV_GN_EPS = 64e-5
RES_HALF = 0.5
N_MOD = 9
S5_WIDTH = D_MODEL // 2
S5_GROUP = 16
S5_GROUPS = S5_WIDTH // S5_GROUP
S5_STATE = 64
RWKV_WIDTH = D_MODEL - S5_WIDTH
RWKV_HEAD = 64
RWKV_HEADS = RWKV_WIDTH // RWKV_HEAD
RWKV_W_LORA = 64
RWKV_A_LORA = 64
RWKV_G_LORA = 128
RWKV_COLS = 3 * RWKV_WIDTH + RWKV_W_LORA + RWKV_A_LORA + RWKV_G_LORA
AB_IN_COLS = S5_WIDTH + RWKV_COLS
HGRN_WIDTH = D_MODEL
HGRN_HEAD = 128
HGRN_HEADS = HGRN_WIDTH // HGRN_HEAD
C_IN_COLS = 4 * HGRN_WIDTH
D_FF = 2816
N_EVEN = (DEPTH + 1) // 2
N_ODD = DEPTH // 2

kernel_name = 'hybrid_streaming_s5_rwkv7_hgrn2_step'


def rmsnorm(x, g):
    xf = x.astype(jnp.float32)
    y = xf * lax.rsqrt(jnp.mean(xf * xf, axis=-1, keepdims=True) + NORM_EPS)
    return (y * g.astype(jnp.float32)).astype(x.dtype)


def modulate(h, shift, scale):
    return h * (1 + scale[:, None, :]) + shift[:, None, :]


def swiglu(h, w1, w3, w2):
    return (jax.nn.silu(h @ w1) * (h @ w3)) @ w2


def _complex_affine_combine(e1, e2):
    a1r, a1i, b1r, b1i = e1
    a2r, a2i, b2r, b2i = e2
    return (a1r * a2r - a1i * a2i, a1r * a2i + a1i * a2r,
            a2r * b1r - a2i * b1i + b2r, a2r * b1i + a2i * b1r + b2i)


def s5_mixer(u, h0_re, h0_im, lam_re, lam_im, log_dt, b_re, b_im, c_re, c_im, d_skip, w_glu):
    f32 = jnp.float32
    bsz, L, _ = u.shape
    uf = u.astype(f32).reshape(bsz, L, S5_GROUPS, S5_GROUP)
    dt = jnp.exp(log_dt.astype(f32))[:, None]
    lr = lam_re.astype(f32)
    li = lam_im.astype(f32)
    mag = jnp.exp(lr * dt)
    ab_re = mag * jnp.cos(li * dt)
    ab_im = mag * jnp.sin(li * dt)
    den = lr * lr + li * li
    z_re = ((ab_re - 1.0) * lr + ab_im * li) / den
    z_im = (ab_im * lr - (ab_re - 1.0) * li) / den
    br = b_re.astype(f32)
    bi = b_im.astype(f32)
    bb_re = z_re[..., None] * br - z_im[..., None] * bi
    bb_im = z_re[..., None] * bi + z_im[..., None] * br
    bu_re = jnp.einsum('blgh,gph->lbgp', uf, bb_re)
    bu_im = jnp.einsum('blgh,gph->lbgp', uf, bb_im)
    h0r = h0_re.astype(f32)
    h0i = h0_im.astype(f32)
    bu_re = bu_re.at[0].add(ab_re * h0r - ab_im * h0i)
    bu_im = bu_im.at[0].add(ab_re * h0i + ab_im * h0r)
    a_re = jnp.broadcast_to(ab_re, (L, 1) + ab_re.shape)
    a_im = jnp.broadcast_to(ab_im, (L, 1) + ab_im.shape)
    _, _, xr, xi = lax.associative_scan(_complex_affine_combine, (a_re, a_im, bu_re, bu_im), axis=0)
    y = (jnp.einsum('lbgp,ghp->blgh', xr, c_re.astype(f32))
         - jnp.einsum('lbgp,ghp->blgh', xi, c_im.astype(f32))
         + uf * d_skip.astype(f32))
    y = jax.nn.gelu(y.reshape(bsz, L, S5_WIDTH))
    y = y * jax.nn.sigmoid(y @ w_glu.astype(f32))
    return y.astype(u.dtype), xr[-1], xi[-1]


def _rwkv_step(S, inp):
    r_t, k_t, v_t, w_t, a_t, b_t = inp
    sa = jnp.einsum('bhvk,bhk->bhv', S, a_t)
    S = S * w_t[:, :, None, :] + sa[..., None] * b_t[:, :, None, :] + v_t[..., None] * k_t[:, :, None, :]
    return S, jnp.einsum('bhvk,bhk->bhv', S, r_t)


def rwkv7_mixer(p, shift_prev, s0, mu, w0, w_w2, a0, w_a2, w_g2, k_k, k_a, r_k, lnx_g, lnx_b):
    f32 = jnp.float32
    bsz, L, _ = p.shape
    W = RWKV_WIDTH
    p_prev = jnp.concatenate([shift_prev[:, None, :].astype(p.dtype), p[:, :-1]], axis=1)
    ps = p + (p_prev - p) * mu
    r, k, v, w_low, a_low, g_low = jnp.split(
        ps, [W, 2 * W, 3 * W, 3 * W + RWKV_W_LORA, 3 * W + RWKV_W_LORA + RWKV_A_LORA], axis=-1)
    logw = -jax.nn.softplus(-(w0 + jnp.tanh(w_low) @ w_w2).astype(f32)) - 0.5
    decay = jnp.exp(-jnp.exp(logw))
    a = jax.nn.sigmoid((a0 + a_low @ w_a2).astype(f32))
    g = jax.nn.sigmoid(g_low) @ w_g2
    hd = (bsz, L, RWKV_HEADS, RWKV_HEAD)
    kk = (k * k_k).astype(f32).reshape(hd)
    kk = kk / jnp.maximum(jnp.linalg.norm(kk, axis=-1, keepdims=True), 1e-12)
    k = (k.astype(f32) * (1 + (a - 1) * k_a.astype(f32))).reshape(hd)
    r = r.astype(f32).reshape(hd)
    v = v.astype(f32).reshape(hd)
    a_h = a.reshape(hd)
    decay = decay.reshape(hd)
    tm = lambda t: jnp.moveaxis(t, 1, 0)
    S, ys = lax.scan(_rwkv_step, s0.astype(f32),
                     (tm(r), tm(k), tm(v), tm(decay), tm(-kk), tm(kk * a_h)))
    y = jnp.moveaxis(ys, 0, 1)
    mean = jnp.mean(y, axis=-1, keepdims=True)
    var = jnp.mean(jnp.square(y - mean), axis=-1, keepdims=True)
    y = ((y - mean) * lax.rsqrt(var + RWKV_GN_EPS)).reshape(bsz, L, W) * lnx_g + lnx_b
    bonus = jnp.sum(r * k * r_k.astype(f32), axis=-1, keepdims=True) * v
    y = (y + bonus.reshape(bsz, L, W)) * g
    return y.astype(p.dtype), p[:, -1], S


def gated_linear_chunked(q, k, v, log_f, s0):
    bsz, L, H, K = q.shape
    n = -(-L // CHUNK)
    pad = n * CHUNK - L
    padw = ((0, 0), (0, pad), (0, 0), (0, 0))
    q, k, v, log_f = [jnp.pad(t, padw) for t in (q, k, v, log_f)]
    blocks = lambda t: t.reshape(bsz, n, CHUNK, H, t.shape[-1]).transpose(1, 0, 3, 2, 4)
    tri = jnp.tril(jnp.ones((CHUNK, CHUNK), dtype=bool))[:, :, None]

    def step(S, inp):
        qc, kc, vc, gc = inp
        b = jnp.cumsum(gc, axis=2)
        diff = jnp.where(tri, b[:, :, :, None, :] - b[:, :, None, :, :], -jnp.inf)
        att = jnp.einsum('bhtk,bhtsk,bhsk->bhts', qc, jnp.exp(diff), kc)
        o = jnp.einsum('bhts,bhsv->bhtv', att, vc) + jnp.einsum('bhtk,bhkv->bhtv', qc * jnp.exp(b), S)
        b_last = b[:, :, -1:, :]
        S = S * jnp.exp(b_last[:, :, 0, :, None]) + jnp.einsum('bhsk,bhsv->bhkv', kc * jnp.exp(b_last - b), vc)
        return S, o

    S, o = lax.scan(step, s0, (blocks(q), blocks(k), blocks(v), blocks(log_f)))
    o = o.transpose(1, 0, 3, 2, 4).reshape(bsz, n * CHUNK, H, v.shape[-1])[:, :L]
    return o, S


def hgrn2_mixer(p, s0, lb, gnorm):
    f32 = jnp.float32
    bsz, L, _ = p.shape
    q, f, v, g = jnp.split(p, 4, axis=-1)
    q = jax.nn.silu(q).astype(f32)
    f = f.astype(f32)
    log_f = jnp.logaddexp(jnp.log(lb), jnp.log1p(-lb) + jax.nn.log_sigmoid(f))
    k = (1 - lb) * jax.nn.sigmoid(-f)
    hd = (bsz, L, HGRN_HEADS, HGRN_HEAD)
    o, S = gated_linear_chunked(q.reshape(hd), k.reshape(hd), v.astype(f32).reshape(hd),
                                log_f.reshape(hd), s0.astype(f32))
    o = rmsnorm(o, gnorm).reshape(bsz, L, HGRN_WIDTH) * jax.nn.silu(g.astype(f32))
    return o.astype(p.dtype), S


def trunk(x, c, s5_re, s5_im, rwkv_s, rwkv_shift, hgrn_s, P):
    lb_all = jax.nn.softmax(P['hgrn_lower_bounds'].astype(jnp.float32), axis=0)
    lb_all = jnp.cumsum(lb_all, axis=0) - lb_all[0]
    cs = jax.nn.silu(c)
    n_re, n_im, n_rw, n_sh, n_hg = [], [], [], [], []
    for l in range(DEPTH):
        mods = cs @ P['w_ada'][l] + P['b_ada'][l]
        sh1, sc1, g1, sh2, sc2, g2, sh3, sc3, g3 = jnp.split(mods, N_MOD, axis=-1)
        h = modulate(rmsnorm(x, P['norm_ffn1'][l]), sh1, sc1)
        x = x + RES_HALF * (1 + g1[:, None]) * swiglu(h, P['ffn1_w1'][l], P['ffn1_w3'][l], P['ffn1_w2'][l])
        h = modulate(rmsnorm(x, P['norm_mix'][l]), sh2, sc2)
        i = l // 2
        if l % 2 == 0:
            p = h @ P['ab_w_in'][i]
            ya, hr, hi = s5_mixer(p[..., :S5_WIDTH], s5_re[i], s5_im[i], P['s5_lam_re'][i], P['s5_lam_im'][i],
                                  P['s5_log_dt'][i], P['s5_b_re'][i], P['s5_b_im'][i], P['s5_c_re'][i],
                                  P['s5_c_im'][i], P['s5_d'][i], P['s5_w_glu'][i])
            yb, sh_new, rw_new = rwkv7_mixer(p[..., S5_WIDTH:], rwkv_shift[i], rwkv_s[i], P['rwkv_mu'][i],
                                             P['rwkv_w0'][i], P['rwkv_w_w2'][i], P['rwkv_a0'][i],
                                             P['rwkv_w_a2'][i], P['rwkv_w_g2'][i], P['rwkv_k_k'][i],
                                             P['rwkv_k_a'][i], P['rwkv_r_k'][i], P['rwkv_lnx_g'][i],
                                             P['rwkv_lnx_b'][i])
            y = jnp.concatenate([ya, yb], axis=-1) @ P['ab_w_out'][i]
            n_re.append(hr)
            n_im.append(hi)
            n_rw.append(rw_new)
            n_sh.append(sh_new)
        else:
            p = h @ P['c_w_in'][i]
            yc, hg_new = hgrn2_mixer(p, hgrn_s[i], lb_all[l], P['hgrn_gnorm'][i])
            y = yc @ P['c_w_out'][i]
            n_hg.append(hg_new)
        x = x + (1 + g2[:, None]) * y
        h = modulate(rmsnorm(x, P['norm_ffn2'][l]), sh3, sc3)
        x = x + RES_HALF * (1 + g3[:, None]) * swiglu(h, P['ffn2_w1'][l], P['ffn2_w3'][l], P['ffn2_w2'][l])
    y = rmsnorm(x, P['final_norm'])
    return (y, jnp.stack(n_re).astype(s5_re.dtype), jnp.stack(n_im).astype(s5_im.dtype),
            jnp.stack(n_rw).astype(rwkv_s.dtype), jnp.stack(n_sh).astype(rwkv_shift.dtype),
            jnp.stack(n_hg).astype(hgrn_s.dtype))


def setup_inputs(seed: int = 0) -> dict:
    key = jax.random.key(seed)
    ks = iter(jax.random.split(key, 64))
    f32 = jnp.float32

    def nrm(shape, scale):
        return jax.random.normal(next(ks), shape, f32) * scale

    def unif(shape, lo, hi):
        return jax.random.uniform(next(ks), shape, f32, lo, hi)

    D = D_MODEL
    G, Pst, H = S5_GROUPS, S5_STATE, S5_GROUP
    W = RWKV_WIDTH
    w0_base = -6.0 + 5.0 * (jnp.arange(W, dtype=f32) / (W - 1)) ** 0.85
    lam_im_base = jnp.pi * jnp.arange(Pst, dtype=f32)
    return {
        'x_prompt': nrm((BATCH, SEQ, D), 1.0),
        'x_sample': nrm((DEC_BATCH, DEC_SEQ, D), 1.0),
        'state_s5_re': nrm((N_EVEN, DEC_BATCH, G, Pst), 0.1),
        'state_s5_im': nrm((N_EVEN, DEC_BATCH, G, Pst), 0.1),
        'state_rwkv': nrm((N_EVEN, DEC_BATCH, RWKV_HEADS, RWKV_HEAD, RWKV_HEAD), 0.1),
        'state_rwkv_shift': nrm((N_EVEN, DEC_BATCH, RWKV_COLS), 1.0),
        'state_hgrn': nrm((N_ODD, DEC_BATCH, HGRN_HEADS, HGRN_HEAD, HGRN_HEAD), 0.5),
        'c_prompt': nrm((BATCH, D), 1.0),
        'c_sample': nrm((DEC_BATCH, D), 1.0),
        'w_ada': nrm((DEPTH, D, N_MOD * D), 0.5 * D ** -0.5),
        'b_ada': nrm((DEPTH, N_MOD * D), 0.01),
        'norm_ffn1': 1.0 + nrm((DEPTH, D), 0.02),
        'norm_mix': 1.0 + nrm((DEPTH, D), 0.02),
        'norm_ffn2': 1.0 + nrm((DEPTH, D), 0.02),
        'ffn1_w1': nrm((DEPTH, D, D_FF), D ** -0.5),
        'ffn1_w3': nrm((DEPTH, D, D_FF), D ** -0.5),
        'ffn1_w2': nrm((DEPTH, D_FF, D), D_FF ** -0.5),
        'ffn2_w1': nrm((DEPTH, D, D_FF), D ** -0.5),
        'ffn2_w3': nrm((DEPTH, D, D_FF), D ** -0.5),
        'ffn2_w2': nrm((DEPTH, D_FF, D), D_FF ** -0.5),
        'ab_w_in': nrm((N_EVEN, D, AB_IN_COLS), D ** -0.5),
        'ab_w_out': nrm((N_EVEN, S5_WIDTH + RWKV_WIDTH, D), (S5_WIDTH + RWKV_WIDTH) ** -0.5),
        's5_lam_re': -0.5 + nrm((N_EVEN, G, Pst), 0.01),
        's5_lam_im': lam_im_base + nrm((N_EVEN, G, Pst), 0.01),
        's5_log_dt': unif((N_EVEN, G), math.log(0.001), math.log(0.1)),
        's5_b_re': nrm((N_EVEN, G, Pst, H), H ** -0.5),
        's5_b_im': nrm((N_EVEN, G, Pst, H), H ** -0.5),
        's5_c_re': nrm((N_EVEN, G, H, Pst), Pst ** -0.5),
        's5_c_im': nrm((N_EVEN, G, H, Pst), Pst ** -0.5),
        's5_d': nrm((N_EVEN, G, H), 1.0),
        's5_w_glu': nrm((N_EVEN, S5_WIDTH, S5_WIDTH), S5_WIDTH ** -0.5),
        'rwkv_mu': unif((N_EVEN, RWKV_COLS), 0.0, 1.0),
        'rwkv_w0': w0_base + nrm((N_EVEN, W), 0.05),
        'rwkv_w_w2': nrm((N_EVEN, RWKV_W_LORA, W), 0.1),
        'rwkv_a0': nrm((N_EVEN, W), 0.1),
        'rwkv_w_a2': nrm((N_EVEN, RWKV_A_LORA, W), 0.1),
        'rwkv_w_g2': nrm((N_EVEN, RWKV_G_LORA, W), RWKV_G_LORA ** -0.5),
        'rwkv_k_k': 0.85 + nrm((N_EVEN, W), 0.02),
        'rwkv_k_a': 1.0 + nrm((N_EVEN, W), 0.02),
        'rwkv_r_k': nrm((N_EVEN, RWKV_HEADS, RWKV_HEAD), 0.1),
        'rwkv_lnx_g': 1.0 + nrm((N_EVEN, W), 0.02),
        'rwkv_lnx_b': nrm((N_EVEN, W), 0.01),
        'c_w_in': nrm((N_ODD, D, C_IN_COLS), D ** -0.5),
        'c_w_out': nrm((N_ODD, HGRN_WIDTH, D), HGRN_WIDTH ** -0.5),
        'hgrn_lower_bounds': nrm((DEPTH, HGRN_WIDTH), 0.1),
        'hgrn_gnorm': 1.0 + nrm((N_ODD, HGRN_HEAD), 0.02),
        'final_norm': 1.0 + nrm((D,), 0.02),
    }


def reference(x_prompt, x_sample, state_s5_re, state_s5_im, state_rwkv, state_rwkv_shift, state_hgrn,
              c_prompt, c_sample, w_ada, b_ada, norm_ffn1, norm_mix, norm_ffn2,
              ffn1_w1, ffn1_w3, ffn1_w2, ffn2_w1, ffn2_w3, ffn2_w2, ab_w_in, ab_w_out,
              s5_lam_re, s5_lam_im, s5_log_dt, s5_b_re, s5_b_im, s5_c_re, s5_c_im, s5_d, s5_w_glu,
              rwkv_mu, rwkv_w0, rwkv_w_w2, rwkv_a0, rwkv_w_a2, rwkv_w_g2, rwkv_k_k, rwkv_k_a, rwkv_r_k,
              rwkv_lnx_g, rwkv_lnx_b, c_w_in, c_w_out, hgrn_lower_bounds, hgrn_gnorm, final_norm):
    P = dict(w_ada=w_ada, b_ada=b_ada, norm_ffn1=norm_ffn1, norm_mix=norm_mix, norm_ffn2=norm_ffn2,
             ffn1_w1=ffn1_w1, ffn1_w3=ffn1_w3, ffn1_w2=ffn1_w2, ffn2_w1=ffn2_w1, ffn2_w3=ffn2_w3,
             ffn2_w2=ffn2_w2, ab_w_in=ab_w_in, ab_w_out=ab_w_out, s5_lam_re=s5_lam_re, s5_lam_im=s5_lam_im,
             s5_log_dt=s5_log_dt, s5_b_re=s5_b_re, s5_b_im=s5_b_im, s5_c_re=s5_c_re, s5_c_im=s5_c_im,
             s5_d=s5_d, s5_w_glu=s5_w_glu, rwkv_mu=rwkv_mu, rwkv_w0=rwkv_w0, rwkv_w_w2=rwkv_w_w2,
             rwkv_a0=rwkv_a0, rwkv_w_a2=rwkv_w_a2, rwkv_w_g2=rwkv_w_g2, rwkv_k_k=rwkv_k_k,
             rwkv_k_a=rwkv_k_a, rwkv_r_k=rwkv_r_k, rwkv_lnx_g=rwkv_lnx_g, rwkv_lnx_b=rwkv_lnx_b,
             c_w_in=c_w_in, c_w_out=c_w_out, hgrn_lower_bounds=hgrn_lower_bounds, hgrn_gnorm=hgrn_gnorm,
             final_norm=final_norm)
    dt = x_prompt.dtype
    bp = x_prompt.shape[0]
    z_re = jnp.zeros((N_EVEN, bp, S5_GROUPS, S5_STATE), dt)
    z_im = jnp.zeros((N_EVEN, bp, S5_GROUPS, S5_STATE), dt)
    z_rw = jnp.zeros((N_EVEN, bp, RWKV_HEADS, RWKV_HEAD, RWKV_HEAD), dt)
    z_sh = jnp.zeros((N_EVEN, bp, RWKV_COLS), dt)
    z_hg = jnp.zeros((N_ODD, bp, HGRN_HEADS, HGRN_HEAD, HGRN_HEAD), dt)
    y_prompt, s5_re_p, s5_im_p, rwkv_p, rwkv_shift_p, hgrn_p = trunk(
        x_prompt, c_prompt, z_re, z_im, z_rw, z_sh, z_hg, P)
    y_sample, s5_re_s, s5_im_s, rwkv_s, rwkv_shift_s, hgrn_s = trunk(
        x_sample, c_sample, state_s5_re, state_s5_im, state_rwkv, state_rwkv_shift, state_hgrn, P)
    return (y_prompt, y_sample, s5_re_p, s5_im_p, rwkv_p, rwkv_shift_p, hgrn_p,
            s5_re_s, s5_im_s, rwkv_s, rwkv_shift_s, hgrn_s)
```

```python
import functools
import math

import jax
import jax.numpy as jnp
from jax import lax
from jax.experimental import pallas as pl
from jax.experimental.pallas import tpu as pltpu

F32 = jnp.float32
BF16 = jnp.bfloat16

NORM_EPS = 1e-6
RWKV_GN_EPS = 64e-5
N_MOD = 9
S5_GROUP = 16
S5_STATE = 64
RWKV_HEAD = 64
HGRN_HEAD = 128
LANES = 128
VMEM_LIMIT = 56 * 1024 * 1024
HGRN_SUB = 16


def _bdot(a, b):
    return jnp.dot(a.astype(BF16), b.astype(BF16), preferred_element_type=F32)


def _bdot_nt(a, b):
    return lax.dot_general(a.astype(BF16), b.astype(BF16), (((1,), (1,)), ((), ())),
                           preferred_element_type=F32)


def _bdot_tn(a, b):
    return lax.dot_general(a.astype(BF16), b.astype(BF16), (((0,), (0,)), ((), ())),
                           preferred_element_type=F32)


def _hdot(a, b):
    return jnp.dot(a, b, preferred_element_type=F32, precision=lax.Precision.HIGHEST)


def _dot2(a, b):
    hi = a.astype(BF16)
    lo = (a - hi.astype(F32)).astype(BF16)
    return (jnp.dot(hi, b, preferred_element_type=F32) + jnp.dot(lo, b, preferred_element_type=F32))


def _sigmoid(x):
    return 1.0 / (1.0 + jnp.exp(-x))


def _silu(x):
    return x * _sigmoid(x)


def _softplus(x):
    return jnp.maximum(x, 0.0) + jnp.log(1.0 + jnp.exp(-jnp.abs(x)))


def _shift_rows(x, s, rows):
    return jnp.where(rows >= s, pltpu.roll(x, s, axis=0), 0.0)


def _cparams(sem):
    return pltpu.CompilerParams(dimension_semantics=sem, vmem_limit_bytes=VMEM_LIMIT)


def _const_spec(shape):
    nd = len(shape)
    return pl.BlockSpec(shape, lambda *_: (0,) * nd)


def _time_tile(L, want):
    t = min(L, want)
    assert L % t == 0 and t % 8 == 0
    return t


def _ada_kernel(c_ref, w_ref, b_ref, o_ref):
    cs = _silu(c_ref[...])
    o_ref[0] = _bdot(cs, w_ref[0]) + b_ref[0]


def _ada(c_all, w_ada, b_ada):
    depth, d, n = w_ada.shape
    nb = c_all.shape[0]
    tn = 1024
    return pl.pallas_call(
        _ada_kernel,
        out_shape=jax.ShapeDtypeStruct((depth, nb, n), F32),
        grid=(depth, n // tn),
        in_specs=[pl.BlockSpec((nb, d), lambda l, j: (0, 0)),
                  pl.BlockSpec((1, d, tn), lambda l, j: (l, 0, j)),
                  pl.BlockSpec((1, 1, tn), lambda l, j: (l, 0, j))],
        out_specs=pl.BlockSpec((1, nb, tn), lambda l, j: (l, 0, j)),
        compiler_params=_cparams(("arbitrary", "arbitrary")),
        name="ada_mod",
    )(c_all, w_ada, b_ada.reshape(depth, 1, n))


def _norm_mod(x, gain, shift, scale):
    y = x * lax.rsqrt(jnp.mean(x * x, axis=-1, keepdims=True) + NORM_EPS) * gain
    return y * (1.0 + scale) + shift


def _ffn_kernel(x_ref, mod_ref, g_ref, w1_ref, w3_ref, w2_ref, *rest, which, fc, final):
    if final:
        gf_ref, o_ref = rest
    else:
        (o_ref,) = rest
    x = x_ref[0]
    shift = mod_ref[0, 3 * which:3 * which + 1, :]
    scale = mod_ref[0, 3 * which + 1:3 * which + 2, :]
    gate = mod_ref[0, 3 * which + 2:3 * which + 3, :]
    h = _norm_mod(x, g_ref[...], shift, scale).astype(BF16)
    d_ff = w1_ref.shape[1]
    acc = jnp.zeros(x.shape, F32)
    for c in range(d_ff // fc):
        a = jnp.dot(h, w1_ref[:, c * fc:(c + 1) * fc], preferred_element_type=F32)
        b = jnp.dot(h, w3_ref[:, c * fc:(c + 1) * fc], preferred_element_type=F32)
        u = (_silu(a) * b).astype(BF16)
        acc = acc + jnp.dot(u, w2_ref[c * fc:(c + 1) * fc, :], preferred_element_type=F32)
    y = x + 0.5 * (1.0 + gate) * acc
    if final:
        y = y * lax.rsqrt(jnp.mean(y * y, axis=-1, keepdims=True) + NORM_EPS) * gf_ref[...]
    o_ref[0] = y


def _ffn(x, mods, gain, w1, w3, w2, which, final_gain=None):
    B, L, D = x.shape
    d_ff = w1.shape[1]
    tm = _time_tile(L, 512)
    fc = 256
    assert d_ff % fc == 0
    final = final_gain is not None
    ins = [x, mods, gain.reshape(1, D), w1, w3, w2]
    specs = [pl.BlockSpec((1, tm, D), lambda b, i: (b, i, 0)),
             pl.BlockSpec((1, N_MOD, D), lambda b, i: (b, 0, 0)),
             _const_spec((1, D)), _const_spec((D, d_ff)), _const_spec((D, d_ff)), _const_spec((d_ff, D))]
    if final:
        ins.append(final_gain.reshape(1, D))
        specs.append(_const_spec((1, D)))
    return pl.pallas_call(
        functools.partial(_ffn_kernel, which=which, fc=fc, final=final),
        out_shape=jax.ShapeDtypeStruct((B, L, D), F32),
        grid=(B, L // tm),
        in_specs=specs,
        out_specs=pl.BlockSpec((1, tm, D), lambda b, i: (b, i, 0)),
        compiler_params=_cparams(("parallel", "parallel")),
        name="ffn",
    )(*ins)


def _inproj_kernel(x_ref, mod_ref, g_ref, w_ref, *o_refs, splits):
    x = x_ref[0]
    h = _norm_mod(x, g_ref[...], mod_ref[0, 3:4, :], mod_ref[0, 4:5, :]).astype(BF16)
    off = 0
    for o_ref, n in zip(o_refs, splits):
        o_ref[0] = jnp.dot(h, w_ref[:, off:off + n], preferred_element_type=F32)
        off += n


def _inproj(x, mods, gain, w_in, splits):
    B, L, D = x.shape
    n = w_in.shape[1]
    assert sum(splits) == n
    tm = _time_tile(L, 512)
    return pl.pallas_call(
        functools.partial(_inproj_kernel, splits=splits),
        out_shape=[jax.ShapeDtypeStruct((B, L, s), F32) for s in splits],
        grid=(B, L // tm),
        in_specs=[pl.BlockSpec((1, tm, D), lambda b, i: (b, i, 0)),
                  pl.BlockSpec((1, N_MOD, D), lambda b, i: (b, 0, 0)),
                  _const_spec((1, D)), _const_spec((D, n))],
        out_specs=[pl.BlockSpec((1, tm, s), lambda b, i: (b, i, 0)) for s in splits],
        compiler_params=_cparams(("parallel", "parallel")),
        name="mixer_in_proj",
    )(x, mods, gain.reshape(1, D), w_in)


def _outproj_kernel(*refs, n_y):
    y_refs = refs[:n_y]
    x_ref, mod_ref, w_ref, o_ref = refs[n_y:]
    off = 0
    acc = None
    for y_ref in y_refs:
        k = y_ref.shape[2]
        part = jnp.dot(y_ref[0], w_ref[off:off + k, :], preferred_element_type=F32)
        acc = part if acc is None else acc + part
        off += k
    o_ref[0] = x_ref[0] + (1.0 + mod_ref[0, 5:6, :]) * acc


def _outproj(ys, x, mods, w_out):
    B, L, D = x.shape
    tm = _time_tile(L, 512)
    k = w_out.shape[0]
    assert sum(y.shape[2] for y in ys) == k
    return pl.pallas_call(
        functools.partial(_outproj_kernel, n_y=len(ys)),
        out_shape=jax.ShapeDtypeStruct((B, L, D), F32),
        grid=(B, L // tm),
        in_specs=[pl.BlockSpec((1, tm, y.shape[2]), lambda b, i: (b, i, 0)) for y in ys]
        + [pl.BlockSpec((1, tm, D), lambda b, i: (b, i, 0)),
           pl.BlockSpec((1, N_MOD, D), lambda b, i: (b, 0, 0)),
           _const_spec((k, D))],
        out_specs=pl.BlockSpec((1, tm, D), lambda b, i: (b, i, 0)),
        compiler_params=_cparams(("parallel", "parallel")),
        name="mixer_out_proj",
    )(*ys, x, mods, w_out)


def _s5_kernel(u_ref, h0_ref, wb_ref, wc_ref, ar_ref, ai_ref, d_ref, wg_ref, y_ref, hT_ref, carry_ref,
               *, n_steps):
    t = pl.program_id(1)

    @pl.when(t == 0)
    def _():
        carry_ref[...] = h0_ref[0]

    u = u_ref[0]
    T = u.shape[0]
    nblk = u.shape[1] // LANES
    half = wb_ref.shape[2] // 2
    rows = lax.broadcasted_iota(jnp.int32, (T, 1), 0)
    ys = []
    for j in range(nblk):
        bu = _bdot(u[:, j * LANES:(j + 1) * LANES], wb_ref[j])
        xr = bu[:, :half]
        xi = bu[:, half:]
        cr = carry_ref[:, 2 * j * half:(2 * j + 1) * half]
        ci = carry_ref[:, (2 * j + 1) * half:(2 * j + 2) * half]
        a1r = ar_ref[0:1, j * half:(j + 1) * half]
        a1i = ai_ref[0:1, j * half:(j + 1) * half]
        first = rows == 0
        xr = xr + jnp.where(first, a1r * cr - a1i * ci, 0.0)
        xi = xi + jnp.where(first, a1r * ci + a1i * cr, 0.0)
        for k in range(n_steps):
            s = 1 << k
            akr = ar_ref[k:k + 1, j * half:(j + 1) * half]
            aki = ai_ref[k:k + 1, j * half:(j + 1) * half]
            sr = _shift_rows(xr, s, rows)
            si = _shift_rows(xi, s, rows)
            xr, xi = xr + akr * sr - aki * si, xi + akr * si + aki * sr
        carry_ref[:, 2 * j * half:(2 * j + 1) * half] = xr[T - 1:T, :]
        carry_ref[:, (2 * j + 1) * half:(2 * j + 2) * half] = xi[T - 1:T, :]
        ys.append(_bdot(xr, wc_ref[j, :half, :]) + _bdot(xi, wc_ref[j, half:, :]))
    y = jnp.concatenate(ys, axis=1) + u * d_ref[...]
    y = jax.nn.gelu(y)
    y_ref[0] = (y * _sigmoid(_bdot(y, wg_ref[...]))).astype(y_ref.dtype)
    hT_ref[0] = carry_ref[...]


def _s5_tables(lam_re, lam_im, log_dt, b_re, b_im, c_re, c_im, n_steps):
    G, P = lam_re.shape
    H = b_re.shape[2]
    gpb = LANES // H
    nblk = G // gpb
    dt = jnp.exp(log_dt)[:, None]
    mag = jnp.exp(lam_re * dt)
    ab_re = mag * jnp.cos(lam_im * dt)
    ab_im = mag * jnp.sin(lam_im * dt)
    den = lam_re * lam_re + lam_im * lam_im
    z_re = ((ab_re - 1.0) * lam_re + ab_im * lam_im) / den
    z_im = (ab_im * lam_re - (ab_re - 1.0) * lam_im) / den
    bb_re = z_re[..., None] * b_re - z_im[..., None] * b_im
    bb_im = z_re[..., None] * b_im + z_im[..., None] * b_re
    eye = jnp.eye(gpb, dtype=F32)

    def blockdiag_in(bb):
        bb = bb.reshape(nblk, gpb, P, H)
        m = jnp.einsum('jgph,gk->jghkp', bb, eye)
        return m.reshape(nblk, gpb * H, gpb * P)

    def blockdiag_out(cc):
        cc = cc.reshape(nblk, gpb, H, P)
        m = jnp.einsum('jghp,gk->jgpkh', cc, eye)
        return m.reshape(nblk, gpb * P, gpb * H)

    wb = jnp.concatenate([blockdiag_in(bb_re), blockdiag_in(bb_im)], axis=2).astype(BF16)
    wc = jnp.concatenate([blockdiag_out(c_re), blockdiag_out(-c_im)], axis=1).astype(BF16)
    pw = (2.0 ** jnp.arange(n_steps, dtype=F32))[:, None, None]
    mag_k = jnp.exp(pw * (lam_re * dt)[None])
    ar = (mag_k * jnp.cos(pw * (lam_im * dt)[None])).reshape(n_steps, G * P)
    ai = (mag_k * jnp.sin(pw * (lam_im * dt)[None])).reshape(n_steps, G * P)
    return wb, wc, ar, ai


def _s5_state_to_cols(h_re, h_im, nblk):
    B, G, P = h_re.shape
    gpb = G // nblk
    st = jnp.stack([h_re.reshape(B, nblk, gpb * P), h_im.reshape(B, nblk, gpb * P)], axis=2)
    return st.reshape(B, 1, 2 * G * P)


def _s5_cols_to_state(cols, G, P, nblk):
    B = cols.shape[0]
    st = cols.reshape(B, nblk, 2, (G // nblk) * P)
    return st[:, :, 0].reshape(B, G, P), st[:, :, 1].reshape(B, G, P)


def _s5(u, h_re, h_im, lam_re, lam_im, log_dt, b_re, b_im, c_re, c_im, d_skip, w_glu_bf16):
    B, L, W = u.shape
    G, P = lam_re.shape
    T = _time_tile(L, 256)
    n_steps = int(math.log2(T))
    assert 1 << n_steps == T
    nblk = W // LANES
    wb, wc, ar, ai = _s5_tables(lam_re, lam_im, log_dt, b_re, b_im, c_re, c_im, max(n_steps, 1))
    h0 = _s5_state_to_cols(h_re, h_im, nblk)
    ncol = 2 * G * P
    y, hT = pl.pallas_call(
        functools.partial(_s5_kernel, n_steps=n_steps),
        out_shape=[jax.ShapeDtypeStruct((B, L, W), BF16), jax.ShapeDtypeStruct((B, 1, ncol), F32)],
        grid=(B, L // T),
        in_specs=[pl.BlockSpec((1, T, W), lambda b, t: (b, t, 0)),
                  pl.BlockSpec((1, 1, ncol), lambda b, t: (b, 0, 0)),
                  _const_spec(wb.shape), _const_spec(wc.shape), _const_spec(ar.shape), _const_spec(ai.shape),
                  _const_spec((1, W)), _const_spec((W, W))],
        out_specs=[pl.BlockSpec((1, T, W), lambda b, t: (b, t, 0)),
                   pl.BlockSpec((1, 1, ncol), lambda b, t: (b, 0, 0))],
        scratch_shapes=[pltpu.VMEM((1, ncol), F32)],
        compiler_params=_cparams(("parallel", "arbitrary")),
        name="s5_mixer",
    )(u, h0, wb, wc, ar, ai, d_skip.reshape(1, W), w_glu_bf16)
    hr, hi = _s5_cols_to_state(hT[:, 0], G, P, nblk)
    return y, hr, hi


def _stack_heads(x, lane_lo):
    return jnp.concatenate([jnp.where(lane_lo, x, 0.0), jnp.where(lane_lo, 0.0, x)], axis=0)


def _rwkv_kernel(p_ref, sp_ref, s0_ref, mu_ref, w0_ref, ww2_ref, a0_ref, wa2_ref, wg2_ref, kk_ref, ka_ref,
                 rk_ref, lng_ref, lnb_ref, hm_ref, y_ref, sh_ref, sT_ref, prev_ref, st_ref, *, width):
    t = pl.program_id(1)
    W = width

    @pl.when(t == 0)
    def _():
        prev_ref[...] = sp_ref[0]
        st_ref[...] = s0_ref[0]

    p = p_ref[0]
    C = p.shape[0]
    rows = lax.broadcasted_iota(jnp.int32, (C, 1), 0)
    p_prev = jnp.where(rows == 0, prev_ref[...], pltpu.roll(p, 1, axis=0))
    prev_ref[...] = p[C - 1:C, :]
    sh_ref[0] = p[C - 1:C, :]
    ps = p + (p_prev - p) * mu_ref[...]
    r = ps[:, :W]
    k = ps[:, W:2 * W]
    v = ps[:, 2 * W:3 * W]
    wa_low = ps[:, 3 * W:3 * W + LANES]
    g_low = ps[:, 3 * W + LANES:]
    xw = w0_ref[...] + _bdot(jnp.tanh(wa_low), ww2_ref[...])
    lw = -jnp.exp(-_softplus(-xw) - 0.5)
    a = _sigmoid(a0_ref[...] + _bdot(wa_low, wa2_ref[...]))
    g = _bdot(_sigmoid(g_low), wg2_ref[...])
    hm = hm_ref[...]
    kk = k * kk_ref[...]
    kk = kk / jnp.maximum(jnp.sqrt(_dot2(kk * kk, hm)), 1e-12)
    kmod = k * (1.0 + (a - 1.0) * ka_ref[...])
    bvec = kk * a

    ci = lax.broadcasted_iota(jnp.int32, (C, C), 0)
    cj = lax.broadcasted_iota(jnp.int32, (C, C), 1)
    cum = _hdot((ci >= cj).astype(F32), lw)
    e_in = jnp.exp(cum)
    e_ex = jnp.exp(cum - lw)
    e_neg = jnp.exp(-cum)
    cum_last = cum[C - 1:C, :]
    e_end = jnp.exp(cum_last - cum)
    w_end = jnp.exp(cum_last)
    at = -kk * e_ex
    rt = r * e_in
    bt = bvec * e_neg
    kt = kmod * e_neg
    bw = bvec * e_end
    kw = kmod * e_end

    C2 = 2 * C
    si = lax.broadcasted_iota(jnp.int32, (C2, C2), 0)
    sj = lax.broadcasted_iota(jnp.int32, (C2, C2), 1)
    strict = si > sj
    lower = si >= sj
    eye = (si == sj).astype(F32)
    lane_lo = lax.broadcasted_iota(jnp.int32, (1, LANES), 1) < RWKV_HEAD
    n_lvl = int(math.log2(C))
    ys = []
    for j in range(W // LANES):
        sl = slice(j * LANES, (j + 1) * LANES)
        ast = _stack_heads(at[:, sl], lane_lo)
        rst = _stack_heads(rt[:, sl], lane_lo)
        bst = _stack_heads(bt[:, sl], lane_lo)
        kst = _stack_heads(kt[:, sl], lane_lo)
        vst = _stack_heads(v[:, sl], lane_lo)
        bwst = _stack_heads(bw[:, sl], lane_lo)
        kwst = _stack_heads(kw[:, sl], lane_lo)
        s0 = st_ref[j]
        ar_st = jnp.concatenate([ast, rst], axis=0)
        m = _bdot_nt(ar_st, jnp.concatenate([bst, kst], axis=0))
        l_ab = jnp.where(strict, m[:C2, :C2], 0.0)
        a_ak = jnp.where(strict, m[:C2, C2:], 0.0)
        a_rb = jnp.where(lower, m[C2:, :C2], 0.0)
        a_rk = jnp.where(lower, m[C2:, C2:], 0.0)
        tinv = eye + l_ab
        lp = l_ab
        for _ in range(n_lvl - 1):
            lp = _hdot(lp, lp)
            tinv = tinv + _hdot(lp, tinv)
        x0 = _bdot_nt(ar_st, s0)
        u_st = _bdot(tinv, x0[:C2] + _bdot(a_ak, vst))
        uv = jnp.concatenate([u_st, vst], axis=0)
        y_st = x0[C2:] + _bdot(jnp.concatenate([a_rb, a_rk], axis=1), uv)
        ys.append(y_st[:C] + y_st[C:])
        st_ref[j] = s0 * w_end[:, sl] + _bdot_tn(uv, jnp.concatenate([bwst, kwst], axis=0))
    y = jnp.concatenate(ys, axis=1)
    inv_n = 1.0 / RWKV_HEAD
    mean = _dot2(y, hm) * inv_n
    yc = y - mean
    var = _dot2(yc * yc, hm) * inv_n
    yn = yc * lax.rsqrt(var + RWKV_GN_EPS) * lng_ref[...] + lnb_ref[...]
    bonus = _dot2(r * kmod * rk_ref[...], hm) * v
    y_ref[0] = ((yn + bonus) * g).astype(y_ref.dtype)
    sT_ref[0] = st_ref[...]


def _rwkv(p, shift_prev, s0, mu, w0, w_w2, a0, w_a2, w_g2, k_k, k_a, r_k, lnx_g, lnx_b):
    B, L, cols = p.shape
    H, N = s0.shape[1], s0.shape[2]
    W = H * N
    C = _time_tile(L, 64)
    assert 1 << int(math.log2(C)) == C and 2 * N == LANES
    npair = H // 2
    s0p = s0.reshape(B, npair, 2, N, N)
    z = jnp.zeros_like(s0p[:, :, 0])
    s0bd = jnp.concatenate([jnp.concatenate([s0p[:, :, 0], z], axis=-1),
                            jnp.concatenate([z, s0p[:, :, 1]], axis=-1)], axis=-2)
    lo = w_w2.shape[0]
    ww2p = jnp.concatenate([w_w2, jnp.zeros((LANES - lo, W), F32)], axis=0).astype(BF16)
    wa2p = jnp.concatenate([jnp.zeros((lo, W), F32), w_a2], axis=0).astype(BF16)
    hid = jnp.arange(W) // N
    hm = (hid[:, None] == hid[None, :]).astype(BF16)
    row = lambda x: x.reshape(1, -1)
    y, sh, sT = pl.pallas_call(
        functools.partial(_rwkv_kernel, width=W),
        out_shape=[jax.ShapeDtypeStruct((B, L, W), BF16),
                   jax.ShapeDtypeStruct((B, 1, cols), F32),
                   jax.ShapeDtypeStruct((B, npair, LANES, LANES), F32)],
        grid=(B, L // C),
        in_specs=[pl.BlockSpec((1, C, cols), lambda b, t: (b, t, 0)),
                  pl.BlockSpec((1, 1, cols), lambda b, t: (b, 0, 0)),
                  pl.BlockSpec((1, npair, LANES, LANES), lambda b, t: (b, 0, 0, 0)),
                  _const_spec((1, cols)), _const_spec((1, W)), _const_spec((LANES, W)), _const_spec((1, W)),
                  _const_spec((LANES, W)), _const_spec((LANES, W)), _const_spec((1, W)), _const_spec((1, W)),
                  _const_spec((1, W)), _const_spec((1, W)), _const_spec((1, W)), _const_spec((W, W))],
        out_specs=[pl.BlockSpec((1, C, W), lambda b, t: (b, t, 0)),
                   pl.BlockSpec((1, 1, cols), lambda b, t: (b, 0, 0)),
                   pl.BlockSpec((1, npair, LANES, LANES), lambda b, t: (b, 0, 0, 0))],
        scratch_shapes=[pltpu.VMEM((1, cols), F32), pltpu.VMEM((npair, LANES, LANES), F32)],
        compiler_params=_cparams(("parallel", "arbitrary")),
        name="rwkv7_mixer",
    )(p, shift_prev.reshape(B, 1, cols), s0bd, row(mu), row(w0), ww2p, row(a0), wa2p, w_g2.astype(BF16),
      row(k_k), row(k_a), row(r_k), row(lnx_g), row(lnx_b), hm)
    sT = sT.reshape(B, npair, 2, N, 2, N)
    s_new = jnp.stack([sT[:, :, 0, :, 0, :], sT[:, :, 1, :, 1, :]], axis=2).reshape(B, H, N, N)
    return y, sh[:, 0], s_new


def _hgrn_kernel(p_ref, s0_ref, llb_ref, l1lb_ref, omlb_ref, gn_ref, y_ref, sT_ref, st_ref, *, width, sub):
    t = pl.program_id(1)
    W = width
    H = W // HGRN_HEAD

    @pl.when(t == 0)
    def _():
        for h in range(H):
            st_ref[h] = s0_ref[0, h].T

    p = p_ref[0]
    C = p.shape[0]
    q = _silu(p[:, :W])
    f = p[:, W:2 * W]
    v = p[:, 2 * W:3 * W]
    g = p[:, 3 * W:]
    lsig = jnp.minimum(f, 0.0) - jnp.log(1.0 + jnp.exp(-jnp.abs(f)))
    c1 = llb_ref[...]
    c2 = l1lb_ref[...] + lsig
    log_f = jnp.maximum(c1, c2) + jnp.log(1.0 + jnp.exp(-jnp.abs(c1 - c2)))
    kv = omlb_ref[...] * _sigmoid(-f)
    ci = lax.broadcasted_iota(jnp.int32, (C, C), 0)
    cj = lax.broadcasted_iota(jnp.int32, (C, C), 1)
    b = _hdot((ci >= cj).astype(F32), log_f)
    b_last = b[C - 1:C, :]
    qe = q * jnp.exp(b)
    ke = kv * jnp.exp(b_last - b)
    e_last = jnp.exp(b_last)
    srow = lax.broadcasted_iota(jnp.int32, (sub, 1), 0)
    nsub = C // sub
    outs = []
    for h in range(H):
        sl = slice(h * HGRN_HEAD, (h + 1) * HGRN_HEAD)
        bh, qh, kh, vh = b[:, sl], q[:, sl], kv[:, sl], v[:, sl]
        sT = st_ref[h]
        o_inter = _bdot_nt(qe[:, sl], sT)
        blocks = []
        for i in range(nsub):
            r0 = i * sub
            bb, qb, kb, vb = bh[r0:r0 + sub], qh[r0:r0 + sub], kh[r0:r0 + sub], vh[r0:r0 + sub]
            o_blk = o_inter[r0:r0 + sub]
            if i > 0:
                b_ref0 = bh[r0 - 1:r0]
                qs = qb * jnp.exp(bb - b_ref0)
                ks = kh[:r0] * jnp.exp(b_ref0 - bh[:r0])
                o_blk = o_blk + _bdot(_bdot_nt(qs, ks), vh[:r0])
            for s in range(sub):
                e = jnp.exp(jnp.where(srow >= s, bb - bb[s:s + 1], -jnp.inf))
                c = jnp.sum(e * qb * kb[s:s + 1], axis=-1, keepdims=True)
                o_blk = o_blk + c * vb[s:s + 1]
            blocks.append(o_blk)
        o = jnp.concatenate(blocks, axis=0)
        st_ref[h] = sT * e_last[:, sl] + _bdot_tn(vh, ke[:, sl])
        o = o * lax.rsqrt(jnp.mean(o * o, axis=-1, keepdims=True) + NORM_EPS) * gn_ref[...]
        outs.append(o)
    o = jnp.concatenate(outs, axis=1)
    y_ref[0] = (o * _silu(g)).astype(y_ref.dtype)

    @pl.when(t == pl.num_programs(1) - 1)
    def _():
        for h in range(H):
            sT_ref[0, h] = st_ref[h].T


def _hgrn(p, s0, lb, gnorm):
    B, L, cols = p.shape
    W = cols // 4
    H = W // HGRN_HEAD
    C = _time_tile(L, 64)
    sub = min(HGRN_SUB, C)
    assert C % sub == 0
    row = lambda x: x.reshape(1, -1)
    y, sT = pl.pallas_call(
        functools.partial(_hgrn_kernel, width=W, sub=sub),
        out_shape=[jax.ShapeDtypeStruct((B, L, W), BF16),
                   jax.ShapeDtypeStruct((B, H, HGRN_HEAD, HGRN_HEAD), F32)],
        grid=(B, L // C),
        in_specs=[pl.BlockSpec((1, C, cols), lambda b, t: (b, t, 0)),
                  pl.BlockSpec((1, H, HGRN_HEAD, HGRN_HEAD), lambda b, t: (b, 0, 0, 0)),
                  _const_spec((1, W)), _const_spec((1, W)), _const_spec((1, W)), _const_spec((1, HGRN_HEAD))],
        out_specs=[pl.BlockSpec((1, C, W), lambda b, t: (b, t, 0)),
                   pl.BlockSpec((1, H, HGRN_HEAD, HGRN_HEAD), lambda b, t: (b, 0, 0, 0))],
        scratch_shapes=[pltpu.VMEM((H, HGRN_HEAD, HGRN_HEAD), F32)],
        compiler_params=_cparams(("parallel", "arbitrary")),
        name="hgrn2_mixer",
    )(p, s0, row(jnp.log(lb)), row(jnp.log1p(-lb)), row(1.0 - lb), row(gnorm))
    return y, sT


def _trunk(x, mods_all, s5_re, s5_im, rwkv_s, rwkv_shift, hgrn_s, P, Pb):
    depth = P['norm_mix'].shape[0]
    lb_all = jax.nn.softmax(P['hgrn_lower_bounds'], axis=0)
    lb_all = jnp.cumsum(lb_all, axis=0) - lb_all[0]
    s5_width = P['s5_w_glu'].shape[1]
    n_re, n_im, n_rw, n_sh, n_hg = [], [], [], [], []
    for l in range(depth):
        mods = mods_all[l]
        x = _ffn(x, mods, P['norm_ffn1'][l], Pb['ffn1_w1'][l], Pb['ffn1_w3'][l], Pb['ffn1_w2'][l], 0)
        i = l // 2
        if l % 2 == 0:
            n_in = Pb['ab_w_in'].shape[2]
            u, pr = _inproj(x, mods, P['norm_mix'][l], Pb['ab_w_in'][i], (s5_width, n_in - s5_width))
            ya, hr, hi = _s5(u, s5_re[i], s5_im[i], P['s5_lam_re'][i], P['s5_lam_im'][i], P['s5_log_dt'][i],
                             P['s5_b_re'][i], P['s5_b_im'][i], P['s5_c_re'][i], P['s5_c_im'][i],
                             P['s5_d'][i], Pb['s5_w_glu'][i])
            yb, sh_new, rw_new = _rwkv(pr, rwkv_shift[i], rwkv_s[i], P['rwkv_mu'][i], P['rwkv_w0'][i],
                                       P['rwkv_w_w2'][i], P['rwkv_a0'][i], P['rwkv_w_a2'][i],
                                       P['rwkv_w_g2'][i], P['rwkv_k_k'][i], P['rwkv_k_a'][i],
                                       P['rwkv_r_k'][i], P['rwkv_lnx_g'][i], P['rwkv_lnx_b'][i])
            x = _outproj([ya, yb], x, mods, Pb['ab_w_out'][i])
            n_re.append(hr)
            n_im.append(hi)
            n_rw.append(rw_new)
            n_sh.append(sh_new)
        else:
            (pc,) = _inproj(x, mods, P['norm_mix'][l], Pb['c_w_in'][i], (Pb['c_w_in'].shape[2],))
            yc, hg_new = _hgrn(pc, hgrn_s[i], lb_all[l], P['hgrn_gnorm'][i])
            x = _outproj([yc], x, mods, Pb['c_w_out'][i])
            n_hg.append(hg_new)
        fin = P['final_norm'] if l == depth - 1 else None
        x = _ffn(x, mods, P['norm_ffn2'][l], Pb['ffn2_w1'][l], Pb['ffn2_w3'][l], Pb['ffn2_w2'][l], 2,
                 final_gain=fin)
    return (x, jnp.stack(n_re), jnp.stack(n_im), jnp.stack(n_rw), jnp.stack(n_sh), jnp.stack(n_hg))


def kernel(x_prompt, x_sample, state_s5_re, state_s5_im, state_rwkv, state_rwkv_shift, state_hgrn, c_prompt, c_sample, w_ada, b_ada, norm_ffn1, norm_mix, norm_ffn2, ffn1_w1, ffn1_w3, ffn1_w2, ffn2_w1, ffn2_w3, ffn2_w2, ab_w_in, ab_w_out, s5_lam_re, s5_lam_im, s5_log_dt, s5_b_re, s5_b_im, s5_c_re, s5_c_im, s5_d, s5_w_glu, rwkv_mu, rwkv_w0, rwkv_w_w2, rwkv_a0, rwkv_w_a2, rwkv_w_g2, rwkv_k_k, rwkv_k_a, rwkv_r_k, rwkv_lnx_g, rwkv_lnx_b, c_w_in, c_w_out, hgrn_lower_bounds, hgrn_gnorm, final_norm):
    P = dict(norm_ffn1=norm_ffn1, norm_mix=norm_mix, norm_ffn2=norm_ffn2,
             s5_lam_re=s5_lam_re, s5_lam_im=s5_lam_im, s5_log_dt=s5_log_dt, s5_b_re=s5_b_re, s5_b_im=s5_b_im,
             s5_c_re=s5_c_re, s5_c_im=s5_c_im, s5_d=s5_d, s5_w_glu=s5_w_glu, rwkv_mu=rwkv_mu,
             rwkv_w0=rwkv_w0, rwkv_w_w2=rwkv_w_w2, rwkv_a0=rwkv_a0, rwkv_w_a2=rwkv_w_a2, rwkv_w_g2=rwkv_w_g2,
             rwkv_k_k=rwkv_k_k, rwkv_k_a=rwkv_k_a, rwkv_r_k=rwkv_r_k, rwkv_lnx_g=rwkv_lnx_g,
             rwkv_lnx_b=rwkv_lnx_b, hgrn_lower_bounds=hgrn_lower_bounds, hgrn_gnorm=hgrn_gnorm,
             final_norm=final_norm)
    Pb = {n: w.astype(BF16) for n, w in dict(
        ffn1_w1=ffn1_w1, ffn1_w3=ffn1_w3, ffn1_w2=ffn1_w2, ffn2_w1=ffn2_w1, ffn2_w3=ffn2_w3, ffn2_w2=ffn2_w2,
        ab_w_in=ab_w_in, ab_w_out=ab_w_out, c_w_in=c_w_in, c_w_out=c_w_out, s5_w_glu=s5_w_glu).items()}
    depth, d = norm_mix.shape
    bp, bs = x_prompt.shape[0], x_sample.shape[0]
    mods = _ada(jnp.concatenate([c_prompt, c_sample], axis=0), w_ada, b_ada)
    mods = mods.reshape(depth, bp + bs, N_MOD, d)
    dt = x_prompt.dtype
    zeros_like_b = lambda s: jnp.zeros((s.shape[0], bp) + s.shape[2:], dt)
    outs_p = _trunk(x_prompt, mods[:, :bp], zeros_like_b(state_s5_re), zeros_like_b(state_s5_im),
                    zeros_like_b(state_rwkv), zeros_like_b(state_rwkv_shift), zeros_like_b(state_hgrn), P, Pb)
    outs_s = _trunk(x_sample, mods[:, bp:], state_s5_re, state_s5_im, state_rwkv, state_rwkv_shift,
                    state_hgrn, P, Pb)
    return (outs_p[0], outs_s[0]) + tuple(outs_p[1:]) + tuple(outs_s[1:])
```

```python
import functools
import math

import jax
import jax.numpy as jnp
from jax import lax
from jax.experimental import pallas as pl
from jax.experimental.pallas import tpu as pltpu

F32 = jnp.float32
BF16 = jnp.bfloat16

NORM_EPS = 1e-6
RWKV_GN_EPS = 64e-5
N_MOD = 9
RWKV_HEAD = 64
HGRN_HEAD = 128
LANES = 128
S5_SEGS = 8
RWKV_SEQS = 4
HGRN_SEQS = 4
VMEM_LIMIT = 56 * 1024 * 1024


def _bdot(a, b):
    return jnp.dot(a.astype(BF16), b.astype(BF16), preferred_element_type=F32)


def _bdot_nt(a, b):
    return lax.dot_general(a.astype(BF16), b.astype(BF16), (((1,), (1,)), ((), ())),
                           preferred_element_type=F32)


def _bdot_tn(a, b):
    return lax.dot_general(a.astype(BF16), b.astype(BF16), (((0,), (0,)), ((), ())),
                           preferred_element_type=F32)


def _dot2(a, b2):
    hi = a.astype(BF16)
    lo = (a - hi.astype(F32)).astype(BF16)
    return jnp.dot(jnp.concatenate([hi, lo], axis=1), b2, preferred_element_type=F32)


def _dot3(m3, x):
    hi = x.astype(BF16)
    r1 = x - hi.astype(F32)
    mid = r1.astype(BF16)
    lo = (r1 - mid.astype(F32)).astype(BF16)
    return jnp.dot(m3, jnp.concatenate([hi, mid, lo], axis=0), preferred_element_type=F32)


def _tile3(m):
    return jnp.concatenate([m, m, m], axis=1)


def _sigmoid(x):
    return 1.0 / (1.0 + jnp.exp(-x))


def _silu(x):
    return x * _sigmoid(x)


def _softplus(x):
    return jnp.maximum(x, 0.0) + jnp.log(1.0 + jnp.exp(-jnp.abs(x)))


def _cparams(sem):
    return pltpu.CompilerParams(dimension_semantics=sem, vmem_limit_bytes=VMEM_LIMIT)


def _const_spec(shape):
    nd = len(shape)
    return pl.BlockSpec(shape, lambda *_: (0,) * nd)


def _time_tile(L, want):
    t = min(L, want)
    assert L % t == 0 and t % 8 == 0
    return t


def _ada_kernel(c_ref, w_ref, b_ref, o_ref):
    cs = _silu(c_ref[...])
    o_ref[0] = _bdot(cs, w_ref[0]) + b_ref[0]


def _ada(c_all, w_ada, b_ada):
    depth, d, n = w_ada.shape
    nb = c_all.shape[0]
    tn = 1024
    return pl.pallas_call(
        _ada_kernel,
        out_shape=jax.ShapeDtypeStruct((depth, nb, n), F32),
        grid=(depth, n // tn),
        in_specs=[pl.BlockSpec((nb, d), lambda l, j: (0, 0)),
                  pl.BlockSpec((1, d, tn), lambda l, j: (l, 0, j)),
                  pl.BlockSpec((1, 1, tn), lambda l, j: (l, 0, j))],
        out_specs=pl.BlockSpec((1, nb, tn), lambda l, j: (l, 0, j)),
        compiler_params=_cparams(("arbitrary", "arbitrary")),
        name="ada_mod",
    )(c_all, w_ada, b_ada.reshape(depth, 1, n))


def _norm_mod(x, gain, shift, scale):
    y = x * lax.rsqrt(jnp.mean(x * x, axis=-1, keepdims=True) + NORM_EPS) * gain
    return y * (1.0 + scale) + shift


def _ffn_kernel(x_ref, mod_ref, g_ref, w1_ref, w3_ref, w2_ref, *rest, which, fc, final):
    if final:
        gf_ref, o_ref = rest
    else:
        (o_ref,) = rest
    x = x_ref[0]
    shift = mod_ref[0, 3 * which:3 * which + 1, :]
    scale = mod_ref[0, 3 * which + 1:3 * which + 2, :]
    gate = mod_ref[0, 3 * which + 2:3 * which + 3, :]
    h = _norm_mod(x, g_ref[...], shift, scale).astype(BF16)
    d_ff = w1_ref.shape[1]
    acc = jnp.zeros(x.shape, F32)
    for c in range(d_ff // fc):
        a = jnp.dot(h, w1_ref[:, c * fc:(c + 1) * fc], preferred_element_type=F32)
        b = jnp.dot(h, w3_ref[:, c * fc:(c + 1) * fc], preferred_element_type=F32)
        u = (_silu(a) * b).astype(BF16)
        acc = acc + jnp.dot(u, w2_ref[c * fc:(c + 1) * fc, :], preferred_element_type=F32)
    y = x + 0.5 * (1.0 + gate) * acc
    if final:
        y = y * lax.rsqrt(jnp.mean(y * y, axis=-1, keepdims=True) + NORM_EPS) * gf_ref[...]
    o_ref[0] = y


def _ffn(x, mods, gain, w1, w3, w2, which, final_gain=None):
    B, L, D = x.shape
    d_ff = w1.shape[1]
    tm = _time_tile(L, 512)
    fc = 256
    assert d_ff % fc == 0
    final = final_gain is not None
    ins = [x, mods, gain.reshape(1, D), w1, w3, w2]
    specs = [pl.BlockSpec((1, tm, D), lambda b, i: (b, i, 0)),
             pl.BlockSpec((1, N_MOD, D), lambda b, i: (b, 0, 0)),
             _const_spec((1, D)), _const_spec((D, d_ff)), _const_spec((D, d_ff)), _const_spec((d_ff, D))]
    if final:
        ins.append(final_gain.reshape(1, D))
        specs.append(_const_spec((1, D)))
    return pl.pallas_call(
        functools.partial(_ffn_kernel, which=which, fc=fc, final=final),
        out_shape=jax.ShapeDtypeStruct((B, L, D), F32),
        grid=(B, L // tm),
        in_specs=specs,
        out_specs=pl.BlockSpec((1, tm, D), lambda b, i: (b, i, 0)),
        compiler_params=_cparams(("parallel", "parallel")),
        name="ffn",
    )(*ins)


def _inproj_kernel(x_ref, mod_ref, g_ref, w_ref, *o_refs, splits):
    x = x_ref[0]
    h = _norm_mod(x, g_ref[...], mod_ref[0, 3:4, :], mod_ref[0, 4:5, :]).astype(BF16)
    off = 0
    for o_ref, n in zip(o_refs, splits):
        o_ref[0] = jnp.dot(h, w_ref[:, off:off + n], preferred_element_type=F32)
        off += n


def _inproj(x, mods, gain, w_in, splits):
    B, L, D = x.shape
    n = w_in.shape[1]
    assert sum(splits) == n
    tm = _time_tile(L, 512)
    return pl.pallas_call(
        functools.partial(_inproj_kernel, splits=splits),
        out_shape=[jax.ShapeDtypeStruct((B, L, s), F32) for s in splits],
        grid=(B, L // tm),
        in_specs=[pl.BlockSpec((1, tm, D), lambda b, i: (b, i, 0)),
                  pl.BlockSpec((1, N_MOD, D), lambda b, i: (b, 0, 0)),
                  _const_spec((1, D)), _const_spec((D, n))],
        out_specs=[pl.BlockSpec((1, tm, s), lambda b, i: (b, i, 0)) for s in splits],
        compiler_params=_cparams(("parallel", "parallel")),
        name="mixer_in_proj",
    )(x, mods, gain.reshape(1, D), w_in)


def _outproj_kernel(*refs, n_y):
    y_refs = refs[:n_y]
    x_ref, mod_ref, w_ref, o_ref = refs[n_y:]
    off = 0
    acc = None
    for y_ref in y_refs:
        k = y_ref.shape[2]
        part = jnp.dot(y_ref[0], w_ref[off:off + k, :], preferred_element_type=F32)
        acc = part if acc is None else acc + part
        off += k
    o_ref[0] = x_ref[0] + (1.0 + mod_ref[0, 5:6, :]) * acc


def _outproj(ys, x, mods, w_out):
    B, L, D = x.shape
    tm = _time_tile(L, 512)
    k = w_out.shape[0]
    assert sum(y.shape[2] for y in ys) == k
    return pl.pallas_call(
        functools.partial(_outproj_kernel, n_y=len(ys)),
        out_shape=jax.ShapeDtypeStruct((B, L, D), F32),
        grid=(B, L // tm),
        in_specs=[pl.BlockSpec((1, tm, y.shape[2]), lambda b, i: (b, i, 0)) for y in ys]
        + [pl.BlockSpec((1, tm, D), lambda b, i: (b, i, 0)),
           pl.BlockSpec((1, N_MOD, D), lambda b, i: (b, 0, 0)),
           _const_spec((k, D))],
        out_specs=pl.BlockSpec((1, tm, D), lambda b, i: (b, i, 0)),
        compiler_params=_cparams(("parallel", "parallel")),
        name="mixer_out_proj",
    )(*ys, x, mods, w_out)


def _s5_kernel(u_ref, h0_ref, perm_ref, permt_ref, wb_ref, wc_ref, a1r_ref, a1i_ref, apr_ref, api_ref,
               asr_ref, asi_ref, d_ref, wg_ref, y_ref, hT_ref, carry_ref, *, tseg):
    t = pl.program_id(1)

    @pl.when(t == 0)
    def _():
        carry_ref[...] = h0_ref[0]

    u = _dot3(perm_ref[...], u_ref[0])
    nblk = u.shape[1] // LANES
    half = wb_ref.shape[2] // 2
    rows8 = lax.broadcasted_iota(jnp.int32, (S5_SEGS, 1), 0)
    n_join = int(math.log2(S5_SEGS))
    ys = []
    for j in range(nblk):
        cols = slice(j * half, (j + 1) * half)
        bu = _bdot(u[:, j * LANES:(j + 1) * LANES], wb_ref[j])
        ar = a1r_ref[:, cols]
        ai = a1i_ref[:, cols]
        sr = jnp.zeros((S5_SEGS, half), F32)
        si = jnp.zeros((S5_SEGS, half), F32)
        xr_rows, xi_rows = [], []
        for i in range(tseg):
            br = bu[S5_SEGS * i:S5_SEGS * (i + 1), :half]
            bi = bu[S5_SEGS * i:S5_SEGS * (i + 1), half:]
            sr, si = ar * sr - ai * si + br, ar * si + ai * sr + bi
            xr_rows.append(sr)
            xi_rows.append(si)
        cr = jnp.where(rows8 == 0, carry_ref[:, 2 * j * half:(2 * j + 1) * half], pltpu.roll(sr, 1, axis=0))
        ci = jnp.where(rows8 == 0, carry_ref[:, (2 * j + 1) * half:(2 * j + 2) * half],
                       pltpu.roll(si, 1, axis=0))
        for m in range(n_join):
            s = 1 << m
            pr = asr_ref[S5_SEGS * m:S5_SEGS * (m + 1), cols]
            pi = asi_ref[S5_SEGS * m:S5_SEGS * (m + 1), cols]
            shr = pltpu.roll(cr, s, axis=0)
            shi = pltpu.roll(ci, s, axis=0)
            cr, ci = cr + pr * shr - pi * shi, ci + pr * shi + pi * shr
        for i in range(tseg):
            pr = apr_ref[S5_SEGS * i:S5_SEGS * (i + 1), cols]
            pi = api_ref[S5_SEGS * i:S5_SEGS * (i + 1), cols]
            xr_rows[i] = xr_rows[i] + pr * cr - pi * ci
            xi_rows[i] = xi_rows[i] + pr * ci + pi * cr
        carry_ref[:, 2 * j * half:(2 * j + 1) * half] = xr_rows[-1][S5_SEGS - 1:, :]
        carry_ref[:, (2 * j + 1) * half:(2 * j + 2) * half] = xi_rows[-1][S5_SEGS - 1:, :]
        xr = jnp.concatenate(xr_rows, axis=0)
        xi = jnp.concatenate(xi_rows, axis=0)
        ys.append(_bdot(xr, wc_ref[j, :half, :]) + _bdot(xi, wc_ref[j, half:, :]))
    y = jnp.concatenate(ys, axis=1) + u * d_ref[...]
    y = jax.nn.gelu(y)
    y = (y * _sigmoid(_bdot(y, wg_ref[...]))).astype(BF16)
    y_ref[0] = jnp.dot(permt_ref[...], y, preferred_element_type=F32).astype(y_ref.dtype)
    hT_ref[0] = carry_ref[...]


def _s5_tables(lam_re, lam_im, log_dt, b_re, b_im, c_re, c_im, tseg):
    G, P = lam_re.shape
    H = b_re.shape[2]
    gpb = LANES // H
    nblk = G // gpb
    dt = jnp.exp(log_dt)[:, None]
    mag = jnp.exp(lam_re * dt)
    ab_re = mag * jnp.cos(lam_im * dt)
    ab_im = mag * jnp.sin(lam_im * dt)
    den = lam_re * lam_re + lam_im * lam_im
    z_re = ((ab_re - 1.0) * lam_re + ab_im * lam_im) / den
    z_im = (ab_im * lam_re - (ab_re - 1.0) * lam_im) / den
    bb_re = z_re[..., None] * b_re - z_im[..., None] * b_im
    bb_im = z_re[..., None] * b_im + z_im[..., None] * b_re
    eye = jnp.eye(gpb, dtype=F32)

    def blockdiag_in(bb):
        bb = bb.reshape(nblk, gpb, P, H)
        m = jnp.einsum('jgph,gk->jghkp', bb, eye)
        return m.reshape(nblk, gpb * H, gpb * P)

    def blockdiag_out(cc):
        cc = cc.reshape(nblk, gpb, H, P)
        m = jnp.einsum('jghp,gk->jgpkh', cc, eye)
        return m.reshape(nblk, gpb * P, gpb * H)

    wb = jnp.concatenate([blockdiag_in(bb_re), blockdiag_in(bb_im)], axis=2).astype(BF16)
    wc = jnp.concatenate([blockdiag_out(c_re), blockdiag_out(-c_im)], axis=1).astype(BF16)

    def powers(n):
        n = n.astype(F32)[:, None, None]
        mg = jnp.exp(n * (lam_re * dt)[None])
        return ((mg * jnp.cos(n * (lam_im * dt)[None])).reshape(-1, G * P),
                (mg * jnp.sin(n * (lam_im * dt)[None])).reshape(-1, G * P))

    a1r, a1i = powers(jnp.ones((S5_SEGS,), jnp.int32))
    apr, api = powers(jnp.repeat(jnp.arange(tseg) + 1, S5_SEGS))
    n_join = int(math.log2(S5_SEGS))
    shifts = jnp.repeat(2 ** jnp.arange(n_join), S5_SEGS)
    asr, asi = powers(tseg * shifts)
    keep = (jnp.tile(jnp.arange(S5_SEGS), n_join) >= shifts)[:, None]
    return wb, wc, a1r, a1i, apr, api, jnp.where(keep, asr, 0.0), jnp.where(keep, asi, 0.0)


def _s5_state_to_cols(h_re, h_im, nblk):
    B, G, P = h_re.shape
    gpb = G // nblk
    st = jnp.stack([h_re.reshape(B, nblk, gpb * P), h_im.reshape(B, nblk, gpb * P)], axis=2)
    return st.reshape(B, 1, 2 * G * P)


def _s5_cols_to_state(cols, G, P, nblk):
    B = cols.shape[0]
    st = cols.reshape(B, nblk, 2, (G // nblk) * P)
    return st[:, :, 0].reshape(B, G, P), st[:, :, 1].reshape(B, G, P)


def _s5(u, h_re, h_im, lam_re, lam_im, log_dt, b_re, b_im, c_re, c_im, d_skip, w_glu_bf16):
    B, L, W = u.shape
    G, P = lam_re.shape
    T = _time_tile(L, 256)
    assert T % S5_SEGS == 0
    tseg = T // S5_SEGS
    nblk = W // LANES
    tables = _s5_tables(lam_re, lam_im, log_dt, b_re, b_im, c_re, c_im, tseg)
    r = jnp.arange(T)
    perm = ((r % S5_SEGS) * tseg + r // S5_SEGS)[:, None] == jnp.arange(T)[None, :]
    perm = perm.astype(BF16)
    h0 = _s5_state_to_cols(h_re, h_im, nblk)
    ncol = 2 * G * P
    consts = (_tile3(perm), perm.T) + tables + (d_skip.reshape(1, W), w_glu_bf16)
    y, hT = pl.pallas_call(
        functools.partial(_s5_kernel, tseg=tseg),
        out_shape=[jax.ShapeDtypeStruct((B, L, W), BF16), jax.ShapeDtypeStruct((B, 1, ncol), F32)],
        grid=(B, L // T),
        in_specs=[pl.BlockSpec((1, T, W), lambda b, t: (b, t, 0)),
                  pl.BlockSpec((1, 1, ncol), lambda b, t: (b, 0, 0))]
        + [_const_spec(c.shape) for c in consts],
        out_specs=[pl.BlockSpec((1, T, W), lambda b, t: (b, t, 0)),
                   pl.BlockSpec((1, 1, ncol), lambda b, t: (b, 0, 0))],
        scratch_shapes=[pltpu.VMEM((1, ncol), F32)],
        compiler_params=_cparams(("parallel", "arbitrary")),
        name="s5_mixer",
    )(u, h0, *consts)
    hr, hi = _s5_cols_to_state(hT[:, 0], G, P, nblk)
    return y, hr, hi


def _stack_heads(x, lane_lo):
    zero = jnp.zeros_like(x)
    return jnp.concatenate([jnp.where(lane_lo, x, zero), jnp.where(lane_lo, zero, x)], axis=0)


def _head_sum(x, hm2):
    nblk = x.shape[1] // LANES
    R = x.shape[0]
    xs = jnp.concatenate([x[:, j * LANES:(j + 1) * LANES] for j in range(nblk)], axis=0)
    s = _dot2(xs, hm2)
    return jnp.concatenate([s[j * R:(j + 1) * R] for j in range(nblk)], axis=1)


def _rwkv_kernel(p_ref, sp_ref, s0_ref, mu_ref, w0_ref, ww2_ref, a0_ref, wa2_ref, wg2_ref, kk_ref, ka_ref,
                 rk_ref, lng_ref, lnb_ref, hm_ref, tri_ref, y_ref, sh_ref, sT_ref, prev_ref, st_ref, *, width):
    t = pl.program_id(1)
    W = width
    NB, C, _ = p_ref.shape

    @pl.when(t == 0)
    def _():
        prev_ref[...] = sp_ref[...]
        st_ref[...] = s0_ref[...]

    rows = lax.broadcasted_iota(jnp.int32, (C, 1), 0)
    ps = []
    for n in range(NB):
        p = p_ref[n]
        p_prev = jnp.where(rows == 0, prev_ref[n], pltpu.roll(p, 1, axis=0))
        prev_ref[n] = p[C - 1:C, :]
        sh_ref[n] = p[C - 1:C, :]
        ps.append(p + (p_prev - p) * mu_ref[...])
    ps = jnp.concatenate(ps, axis=0)
    r = ps[:, :W]
    k = ps[:, W:2 * W]
    v = ps[:, 2 * W:3 * W]
    wa_low = ps[:, 3 * W:3 * W + LANES]
    g_low = ps[:, 3 * W + LANES:]
    xw = w0_ref[...] + _bdot(jnp.tanh(wa_low), ww2_ref[...])
    lw = -jnp.exp(-_softplus(-xw) - 0.5)
    a = _sigmoid(a0_ref[...] + _bdot(wa_low, wa2_ref[...]))
    g = _bdot(_sigmoid(g_low), wg2_ref[...])
    hm = hm_ref[...]
    kk = k * kk_ref[...]
    kk = kk / jnp.maximum(jnp.sqrt(_head_sum(kk * kk, hm)), 1e-12)
    kmod = k * (1.0 + (a - 1.0) * ka_ref[...])
    bvec = kk * a

    cum = _dot3(tri_ref[...], lw)
    e_in = jnp.exp(cum)
    e_ex = jnp.exp(cum - lw)
    e_neg = jnp.exp(-cum)
    cum_last = [cum[(n + 1) * C - 1:(n + 1) * C, :] for n in range(NB)]
    e_end = jnp.exp(jnp.concatenate([jnp.broadcast_to(c, (C, W)) for c in cum_last], axis=0) - cum)
    w_end = [jnp.exp(c) for c in cum_last]
    at = (-kk * e_ex).astype(BF16)
    rt = (r * e_in).astype(BF16)
    bt = (bvec * e_neg).astype(BF16)
    kt = (kmod * e_neg).astype(BF16)
    bw = (bvec * e_end).astype(BF16)
    kw = (kmod * e_end).astype(BF16)
    vb = v.astype(BF16)

    C2 = 2 * C
    si = lax.broadcasted_iota(jnp.int32, (C2, C2), 0)
    sj = lax.broadcasted_iota(jnp.int32, (C2, C2), 1)
    strict = si > sj
    lower = si >= sj
    lane_lo = lax.broadcasted_iota(jnp.int32, (1, LANES), 1) < RWKV_HEAD
    n_lvl = int(math.log2(C))
    npair = W // LANES
    chains = [(n, j) for n in range(NB) for j in range(npair)]
    x0, vst, bkw, l_ab, a_ak, a_r = {}, {}, {}, {}, {}, {}
    for c in chains:
        n, j = c
        pick = lambda x: _stack_heads(x[n * C:(n + 1) * C, j * LANES:(j + 1) * LANES], lane_lo)
        ar_st = jnp.concatenate([pick(at), pick(rt)], axis=0)
        m = _bdot_nt(ar_st, jnp.concatenate([pick(bt), pick(kt)], axis=0))
        x0[c] = _bdot_nt(ar_st, st_ref[n, j])
        vst[c] = pick(vb)
        bkw[c] = jnp.concatenate([pick(bw), pick(kw)], axis=0)
        l_ab[c] = jnp.where(strict, m[:C2, :C2], 0.0)
        a_ak[c] = jnp.where(strict, m[:C2, C2:], 0.0).astype(BF16)
        a_r[c] = jnp.concatenate([jnp.where(lower, m[C2:, :C2], 0.0), jnp.where(lower, m[C2:, C2:], 0.0)],
                                 axis=1).astype(BF16)
    z = {c: x0[c][:C2] + jnp.dot(a_ak[c], vst[c], preferred_element_type=F32) for c in chains}
    lp = {c: l_ab[c].astype(BF16) for c in chains}
    for lvl in range(n_lvl):
        last = lvl == n_lvl - 1
        for c in chains:
            zb = z[c].astype(BF16)
            prod = jnp.dot(lp[c], zb if last else jnp.concatenate([zb, lp[c]], axis=1),
                           preferred_element_type=F32)
            z[c] = z[c] + prod[:, :LANES]
            if not last:
                lp[c] = prod[:, LANES:].astype(BF16)
    uv = {c: jnp.concatenate([z[c].astype(BF16), vst[c]], axis=0) for c in chains}
    y_st = {c: x0[c][C2:] + jnp.dot(a_r[c], uv[c], preferred_element_type=F32) for c in chains}
    for c in chains:
        n, j = c
        st_ref[n, j] = st_ref[n, j] * w_end[n][:, j * LANES:(j + 1) * LANES] + _bdot_tn(uv[c], bkw[c])
    y = jnp.concatenate([jnp.concatenate([y_st[(n, j)][:C] + y_st[(n, j)][C:] for j in range(npair)], axis=1)
                         for n in range(NB)], axis=0)
    inv_n = 1.0 / RWKV_HEAD
    mean = _head_sum(y, hm) * inv_n
    yc = y - mean
    var = _head_sum(yc * yc, hm) * inv_n
    yn = yc * lax.rsqrt(var + RWKV_GN_EPS) * lng_ref[...] + lnb_ref[...]
    bonus = _head_sum(r * kmod * rk_ref[...], hm) * v
    out = ((yn + bonus) * g).astype(y_ref.dtype)
    for n in range(NB):
        y_ref[n] = out[n * C:(n + 1) * C]

    @pl.when(t == pl.num_programs(1) - 1)
    def _():
        sT_ref[...] = st_ref[...]


def _rwkv(p, shift_prev, s0, mu, w0, w_w2, a0, w_a2, w_g2, k_k, k_a, r_k, lnx_g, lnx_b):
    B, L, cols = p.shape
    H, N = s0.shape[1], s0.shape[2]
    W = H * N
    C = _time_tile(L, 64)
    assert 1 << int(math.log2(C)) == C and 2 * N == LANES
    npair = H // 2
    s0p = s0.reshape(B, npair, 2, N, N)
    z = jnp.zeros_like(s0p[:, :, 0])
    s0bd = jnp.concatenate([jnp.concatenate([s0p[:, :, 0], z], axis=-1),
                            jnp.concatenate([z, s0p[:, :, 1]], axis=-1)], axis=-2)
    lo = w_w2.shape[0]
    ww2p = jnp.concatenate([w_w2, jnp.zeros((LANES - lo, W), F32)], axis=0).astype(BF16)
    wa2p = jnp.concatenate([jnp.zeros((lo, W), F32), w_a2], axis=0).astype(BF16)
    hid = jnp.arange(LANES) // N
    hm = (hid[:, None] == hid[None, :]).astype(BF16)
    hm2 = jnp.concatenate([hm, hm], axis=0)
    nb = RWKV_SEQS if B % RWKV_SEQS == 0 else 1
    tr = jnp.arange(nb * C)
    tri = ((tr[:, None] >= tr[None, :]) & (tr[:, None] // C == tr[None, :] // C)).astype(BF16)
    row = lambda x: x.reshape(1, -1)
    y, sh, sT = pl.pallas_call(
        functools.partial(_rwkv_kernel, width=W),
        out_shape=[jax.ShapeDtypeStruct((B, L, W), BF16),
                   jax.ShapeDtypeStruct((B, 1, cols), F32),
                   jax.ShapeDtypeStruct((B, npair, LANES, LANES), F32)],
        grid=(B // nb, L // C),
        in_specs=[pl.BlockSpec((nb, C, cols), lambda b, t: (b, t, 0)),
                  pl.BlockSpec((nb, 1, cols), lambda b, t: (b, 0, 0)),
                  pl.BlockSpec((nb, npair, LANES, LANES), lambda b, t: (b, 0, 0, 0)),
                  _const_spec((1, cols)), _const_spec((1, W)), _const_spec((LANES, W)), _const_spec((1, W)),
                  _const_spec((LANES, W)), _const_spec((LANES, W)), _const_spec((1, W)), _const_spec((1, W)),
                  _const_spec((1, W)), _const_spec((1, W)), _const_spec((1, W)), _const_spec((2 * LANES, LANES)),
                  _const_spec((nb * C, 3 * nb * C))],
        out_specs=[pl.BlockSpec((nb, C, W), lambda b, t: (b, t, 0)),
                   pl.BlockSpec((nb, 1, cols), lambda b, t: (b, 0, 0)),
                   pl.BlockSpec((nb, npair, LANES, LANES), lambda b, t: (b, 0, 0, 0))],
        scratch_shapes=[pltpu.VMEM((nb, 1, cols), F32), pltpu.VMEM((nb, npair, LANES, LANES), F32)],
        compiler_params=_cparams(("parallel", "arbitrary")),
        name="rwkv7_mixer",
    )(p, shift_prev.reshape(B, 1, cols), s0bd, row(mu), row(w0), ww2p, row(a0), wa2p, w_g2.astype(BF16),
      row(k_k), row(k_a), row(r_k), row(lnx_g), row(lnx_b), hm2, _tile3(tri))
    sT = sT.reshape(B, npair, 2, N, 2, N)
    s_new = jnp.stack([sT[:, :, 0, :, 0, :], sT[:, :, 1, :, 1, :]], axis=2).reshape(B, H, N, N)
    return y, sh[:, 0], s_new


def _hgrn_segment_matrix(C):
    t = jnp.arange(C)[:, None]
    r = jnp.arange(C)[None, :]
    blocks = [r <= t]
    h = C // 2
    while h >= 1:
        mid = (t // (2 * h)) * (2 * h) + h - 1
        is_q = (t & h) != 0
        blocks.append(jnp.where(is_q, (r > mid) & (r <= t), (r > t) & (r <= mid)))
        h //= 2
    return jnp.concatenate(blocks, axis=0).astype(BF16)


def _hgrn_kernel(p_ref, s0_ref, llb_ref, l1lb_ref, omlb_ref, gn_ref, mseg_ref, y_ref, sT_ref, st_ref, *, width):
    t = pl.program_id(1)
    W = width
    H = W // HGRN_HEAD
    NB, C, _ = p_ref.shape

    @pl.when(t == 0)
    def _():
        for n in range(NB):
            for h in range(H):
                st_ref[n, h] = s0_ref[n, h].T

    p = jnp.concatenate([p_ref[n] for n in range(NB)], axis=0)
    q = _silu(p[:, :W])
    f = p[:, W:2 * W]
    v = p[:, 2 * W:3 * W]
    g = p[:, 3 * W:]
    ef = jnp.exp(-jnp.abs(f))
    lsig = jnp.minimum(f, 0.0) - jnp.log(1.0 + ef)
    c1 = llb_ref[...]
    c2 = l1lb_ref[...] + lsig
    log_f = jnp.maximum(c1, c2) + jnp.log(1.0 + jnp.exp(-jnp.abs(c1 - c2)))
    kv = omlb_ref[...] * (jnp.where(f >= 0.0, ef, 1.0) / (1.0 + ef))
    d_all = [_dot3(mseg_ref[...], log_f[n * C:(n + 1) * C]) for n in range(NB)]
    b = jnp.concatenate([d[:C] for d in d_all], axis=0)
    b_last = [d[C - 1:C, :] for d in d_all]
    qe = (q * jnp.exp(b)).astype(BF16)
    ke = (kv * jnp.exp(jnp.concatenate([jnp.broadcast_to(bl, (C, W)) for bl in b_last], axis=0) - b)).astype(BF16)
    e_last = [jnp.exp(bl) for bl in b_last]
    qk = q * kv
    vb = v.astype(BF16)
    rowi = lax.broadcasted_iota(jnp.int32, (NB * C, 1), 0)
    ti = lax.broadcasted_iota(jnp.int32, (C, C), 0)
    tj = lax.broadcasted_iota(jnp.int32, (C, C), 1)
    zs, masks = [], []
    hsz = C // 2
    lvl = 1
    while hsz >= 1:
        is_q = (rowi & hsz) != 0
        seg = jnp.concatenate([d[lvl * C:(lvl + 1) * C] for d in d_all], axis=0)
        zs.append((jnp.exp(seg) * jnp.where(is_q, q, kv)).astype(BF16))
        masks.append(((ti & hsz) != 0) & ((tj & hsz) == 0) & ((ti // (2 * hsz)) == (tj // (2 * hsz))))
        hsz //= 2
        lvl += 1
    chains = [(n, h) for n in range(NB) for h in range(H)]
    blk = lambda x, c: x[c[0] * C:(c[0] + 1) * C, c[1] * HGRN_HEAD:(c[1] + 1) * HGRN_HEAD]
    att = {}
    for z, mask in zip(zs, masks):
        for c in chains:
            zh = blk(z, c)
            term = jnp.where(mask, _bdot_nt(zh, zh), 0.0)
            att[c] = term if c not in att else att[c] + term
    o = {c: (_bdot(att[c], blk(vb, c)) + _bdot_nt(blk(qe, c), st_ref[c[0], c[1]])
             + jnp.sum(blk(qk, c), axis=-1, keepdims=True) * blk(v, c)) for c in chains}
    for c in chains:
        n, h = c
        st_ref[n, h] = (st_ref[n, h] * e_last[n][:, h * HGRN_HEAD:(h + 1) * HGRN_HEAD]
                        + _bdot_tn(blk(vb, c), blk(ke, c)))
    for c in chains:
        o[c] = o[c] * lax.rsqrt(jnp.mean(o[c] * o[c], axis=-1, keepdims=True) + NORM_EPS) * gn_ref[...]
    o = jnp.concatenate([jnp.concatenate([o[(n, h)] for h in range(H)], axis=1) for n in range(NB)], axis=0)
    out = (o * _silu(g)).astype(y_ref.dtype)
    for n in range(NB):
        y_ref[n] = out[n * C:(n + 1) * C]

    @pl.when(t == pl.num_programs(1) - 1)
    def _():
        for n in range(NB):
            for h in range(H):
                sT_ref[n, h] = st_ref[n, h].T


def _hgrn(p, s0, lb, gnorm):
    B, L, cols = p.shape
    W = cols // 4
    H = W // HGRN_HEAD
    C = _time_tile(L, 64)
    assert 1 << int(math.log2(C)) == C
    mseg = _tile3(_hgrn_segment_matrix(C))
    nb = HGRN_SEQS if B % HGRN_SEQS == 0 else 1
    row = lambda x: x.reshape(1, -1)
    y, sT = pl.pallas_call(
        functools.partial(_hgrn_kernel, width=W),
        out_shape=[jax.ShapeDtypeStruct((B, L, W), BF16),
                   jax.ShapeDtypeStruct((B, H, HGRN_HEAD, HGRN_HEAD), F32)],
        grid=(B // nb, L // C),
        in_specs=[pl.BlockSpec((nb, C, cols), lambda b, t: (b, t, 0)),
                  pl.BlockSpec((nb, H, HGRN_HEAD, HGRN_HEAD), lambda b, t: (b, 0, 0, 0)),
                  _const_spec((1, W)), _const_spec((1, W)), _const_spec((1, W)), _const_spec((1, HGRN_HEAD)),
                  _const_spec(mseg.shape)],
        out_specs=[pl.BlockSpec((nb, C, W), lambda b, t: (b, t, 0)),
                   pl.BlockSpec((nb, H, HGRN_HEAD, HGRN_HEAD), lambda b, t: (b, 0, 0, 0))],
        scratch_shapes=[pltpu.VMEM((nb, H, HGRN_HEAD, HGRN_HEAD), F32)],
        compiler_params=_cparams(("parallel", "arbitrary")),
        name="hgrn2_mixer",
    )(p, s0, row(jnp.log(lb)), row(jnp.log1p(-lb)), row(1.0 - lb), row(gnorm), mseg)
    return y, sT


def _trunk(x, mods_all, s5_re, s5_im, rwkv_s, rwkv_shift, hgrn_s, P, Pb):
    depth = P['norm_mix'].shape[0]
    lb_all = jax.nn.softmax(P['hgrn_lower_bounds'], axis=0)
    lb_all = jnp.cumsum(lb_all, axis=0) - lb_all[0]
    s5_width = P['s5_w_glu'].shape[1]
    n_re, n_im, n_rw, n_sh, n_hg = [], [], [], [], []
    for l in range(depth):
        mods = mods_all[l]
        x = _ffn(x, mods, P['norm_ffn1'][l], Pb['ffn1_w1'][l], Pb['ffn1_w3'][l], Pb['ffn1_w2'][l], 0)
        i = l // 2
        if l % 2 == 0:
            n_in = Pb['ab_w_in'].shape[2]
            u, pr = _inproj(x, mods, P['norm_mix'][l], Pb['ab_w_in'][i], (s5_width, n_in - s5_width))
            ya, hr, hi = _s5(u, s5_re[i], s5_im[i], P['s5_lam_re'][i], P['s5_lam_im'][i], P['s5_log_dt'][i],
                             P['s5_b_re'][i], P['s5_b_im'][i], P['s5_c_re'][i], P['s5_c_im'][i],
                             P['s5_d'][i], Pb['s5_w_glu'][i])
            yb, sh_new, rw_new = _rwkv(pr, rwkv_shift[i], rwkv_s[i], P['rwkv_mu'][i], P['rwkv_w0'][i],
                                       P['rwkv_w_w2'][i], P['rwkv_a0'][i], P['rwkv_w_a2'][i],
                                       P['rwkv_w_g2'][i], P['rwkv_k_k'][i], P['rwkv_k_a'][i],
                                       P['rwkv_r_k'][i], P['rwkv_lnx_g'][i], P['rwkv_lnx_b'][i])
            x = _outproj([ya, yb], x, mods, Pb['ab_w_out'][i])
            n_re.append(hr)
            n_im.append(hi)
            n_rw.append(rw_new)
            n_sh.append(sh_new)
        else:
            (pc,) = _inproj(x, mods, P['norm_mix'][l], Pb['c_w_in'][i], (Pb['c_w_in'].shape[2],))
            yc, hg_new = _hgrn(pc, hgrn_s[i], lb_all[l], P['hgrn_gnorm'][i])
            x = _outproj([yc], x, mods, Pb['c_w_out'][i])
            n_hg.append(hg_new)
        fin = P['final_norm'] if l == depth - 1 else None
        x = _ffn(x, mods, P['norm_ffn2'][l], Pb['ffn2_w1'][l], Pb['ffn2_w3'][l], Pb['ffn2_w2'][l], 2,
                 final_gain=fin)
    return (x, jnp.stack(n_re), jnp.stack(n_im), jnp.stack(n_rw), jnp.stack(n_sh), jnp.stack(n_hg))


def kernel(x_prompt, x_sample, state_s5_re, state_s5_im, state_rwkv, state_rwkv_shift, state_hgrn, c_prompt, c_sample, w_ada, b_ada, norm_ffn1, norm_mix, norm_ffn2, ffn1_w1, ffn1_w3, ffn1_w2, ffn2_w1, ffn2_w3, ffn2_w2, ab_w_in, ab_w_out, s5_lam_re, s5_lam_im, s5_log_dt, s5_b_re, s5_b_im, s5_c_re, s5_c_im, s5_d, s5_w_glu, rwkv_mu, rwkv_w0, rwkv_w_w2, rwkv_a0, rwkv_w_a2, rwkv_w_g2, rwkv_k_k, rwkv_k_a, rwkv_r_k, rwkv_lnx_g, rwkv_lnx_b, c_w_in, c_w_out, hgrn_lower_bounds, hgrn_gnorm, final_norm):
    P = dict(norm_ffn1=norm_ffn1, norm_mix=norm_mix, norm_ffn2=norm_ffn2,
             s5_lam_re=s5_lam_re, s5_lam_im=s5_lam_im, s5_log_dt=s5_log_dt, s5_b_re=s5_b_re, s5_b_im=s5_b_im,
             s5_c_re=s5_c_re, s5_c_im=s5_c_im, s5_d=s5_d, s5_w_glu=s5_w_glu, rwkv_mu=rwkv_mu,
             rwkv_w0=rwkv_w0, rwkv_w_w2=rwkv_w_w2, rwkv_a0=rwkv_a0, rwkv_w_a2=rwkv_w_a2, rwkv_w_g2=rwkv_w_g2,
             rwkv_k_k=rwkv_k_k, rwkv_k_a=rwkv_k_a, rwkv_r_k=rwkv_r_k, rwkv_lnx_g=rwkv_lnx_g,
             rwkv_lnx_b=rwkv_lnx_b, hgrn_lower_bounds=hgrn_lower_bounds, hgrn_gnorm=hgrn_gnorm,
             final_norm=final_norm)
    Pb = {n: w.astype(BF16) for n, w in dict(
        ffn1_w1=ffn1_w1, ffn1_w3=ffn1_w3, ffn1_w2=ffn1_w2, ffn2_w1=ffn2_w1, ffn2_w3=ffn2_w3, ffn2_w2=ffn2_w2,
        ab_w_in=ab_w_in, ab_w_out=ab_w_out, c_w_in=c_w_in, c_w_out=c_w_out, s5_w_glu=s5_w_glu).items()}
    depth, d = norm_mix.shape
    bp, bs = x_prompt.shape[0], x_sample.shape[0]
    mods = _ada(jnp.concatenate([c_prompt, c_sample], axis=0), w_ada, b_ada)
    mods = mods.reshape(depth, bp + bs, N_MOD, d)
    dt = x_prompt.dtype
    zeros_like_b = lambda s: jnp.zeros((s.shape[0], bp) + s.shape[2:], dt)
    outs_p = _trunk(x_prompt, mods[:, :bp], zeros_like_b(state_s5_re), zeros_like_b(state_s5_im),
                    zeros_like_b(state_rwkv), zeros_like_b(state_rwkv_shift), zeros_like_b(state_hgrn), P, Pb)
    outs_s = _trunk(x_sample, mods[:, bp:], state_s5_re, state_s5_im, state_rwkv, state_rwkv_shift,
                    state_hgrn, P, Pb)
    return (outs_p[0], outs_s[0]) + tuple(outs_p[1:]) + tuple(outs_s[1:])
```

```python
import functools
import math

import jax
import jax.numpy as jnp
from jax import lax
from jax.experimental import pallas as pl
from jax.experimental.pallas import tpu as pltpu

F32 = jnp.float32
BF16 = jnp.bfloat16

NORM_EPS = 1e-6
RWKV_GN_EPS = 64e-5
N_MOD = 9
RWKV_HEAD = 64
RWKV_BASE = 8
HGRN_HEAD = 128
LANES = 128
S5_SEGS = 8
RWKV_SEQS = 4
HGRN_SEQS = 4
VMEM_LIMIT = 56 * 1024 * 1024


def _bdot(a, b):
    return jnp.dot(a.astype(BF16), b.astype(BF16), preferred_element_type=F32)


def _bdot_nt(a, b):
    return lax.dot_general(a.astype(BF16), b.astype(BF16), (((1,), (1,)), ((), ())),
                           preferred_element_type=F32)


def _bdot_tn(a, b):
    return lax.dot_general(a.astype(BF16), b.astype(BF16), (((0,), (0,)), ((), ())),
                           preferred_element_type=F32)


def _dot2(a, b2):
    hi = a.astype(BF16)
    lo = (a - hi.astype(F32)).astype(BF16)
    return jnp.dot(jnp.concatenate([hi, lo], axis=1), b2, preferred_element_type=F32)


def _dot3(m3, x):
    hi = x.astype(BF16)
    r1 = x - hi.astype(F32)
    mid = r1.astype(BF16)
    lo = (r1 - mid.astype(F32)).astype(BF16)
    return jnp.dot(m3, jnp.concatenate([hi, mid, lo], axis=0), preferred_element_type=F32)


def _tile3(m):
    return jnp.concatenate([m, m, m], axis=1)


def _sigmoid(x):
    return 1.0 / (1.0 + jnp.exp(-x))


def _silu(x):
    return x * _sigmoid(x)


def _softplus(x):
    return jnp.maximum(x, 0.0) + jnp.log(1.0 + jnp.exp(-jnp.abs(x)))


def _cparams(sem):
    return pltpu.CompilerParams(dimension_semantics=sem, vmem_limit_bytes=VMEM_LIMIT)


def _const_spec(shape):
    nd = len(shape)
    return pl.BlockSpec(shape, lambda *_: (0,) * nd)


def _time_tile(L, want):
    t = min(L, want)
    assert L % t == 0 and t % 8 == 0
    return t


def _ada_kernel(c_ref, w_ref, b_ref, o_ref):
    cs = _silu(c_ref[...])
    o_ref[0] = _bdot(cs, w_ref[0]) + b_ref[0]


def _ada(c_all, w_ada, b_ada):
    depth, d, n = w_ada.shape
    nb = c_all.shape[0]
    tn = 1024
    return pl.pallas_call(
        _ada_kernel,
        out_shape=jax.ShapeDtypeStruct((depth, nb, n), F32),
        grid=(depth, n // tn),
        in_specs=[pl.BlockSpec((nb, d), lambda l, j: (0, 0)),
                  pl.BlockSpec((1, d, tn), lambda l, j: (l, 0, j)),
                  pl.BlockSpec((1, 1, tn), lambda l, j: (l, 0, j))],
        out_specs=pl.BlockSpec((1, nb, tn), lambda l, j: (l, 0, j)),
        compiler_params=_cparams(("arbitrary", "arbitrary")),
        name="ada_mod",
    )(c_all, w_ada, b_ada.reshape(depth, 1, n))


def _norm_mod(x, gain, shift, scale):
    y = x * lax.rsqrt(jnp.mean(x * x, axis=-1, keepdims=True) + NORM_EPS) * gain
    return y * (1.0 + scale) + shift


def _ffn_kernel(x_ref, mod_ref, g_ref, w1_ref, w3_ref, w2_ref, *rest, which, fc, final):
    if final:
        gf_ref, o_ref = rest
    else:
        (o_ref,) = rest
    x = x_ref[0]
    shift = mod_ref[0, 3 * which:3 * which + 1, :]
    scale = mod_ref[0, 3 * which + 1:3 * which + 2, :]
    gate = mod_ref[0, 3 * which + 2:3 * which + 3, :]
    h = _norm_mod(x, g_ref[...], shift, scale).astype(BF16)
    d_ff = w1_ref.shape[1]
    acc = jnp.zeros(x.shape, F32)
    for c in range(d_ff // fc):
        a = jnp.dot(h, w1_ref[:, c * fc:(c + 1) * fc], preferred_element_type=F32)
        b = jnp.dot(h, w3_ref[:, c * fc:(c + 1) * fc], preferred_element_type=F32)
        u = (_silu(a) * b).astype(BF16)
        acc = acc + jnp.dot(u, w2_ref[c * fc:(c + 1) * fc, :], preferred_element_type=F32)
    y = x + 0.5 * (1.0 + gate) * acc
    if final:
        y = y * lax.rsqrt(jnp.mean(y * y, axis=-1, keepdims=True) + NORM_EPS) * gf_ref[...]
    o_ref[0] = y


def _ffn(x, mods, gain, w1, w3, w2, which, final_gain=None):
    B, L, D = x.shape
    d_ff = w1.shape[1]
    tm = _time_tile(L, 512)
    fc = 256
    assert d_ff % fc == 0
    final = final_gain is not None
    ins = [x, mods, gain.reshape(1, D), w1, w3, w2]
    specs = [pl.BlockSpec((1, tm, D), lambda b, i: (b, i, 0)),
             pl.BlockSpec((1, N_MOD, D), lambda b, i: (b, 0, 0)),
             _const_spec((1, D)), _const_spec((D, d_ff)), _const_spec((D, d_ff)), _const_spec((d_ff, D))]
    if final:
        ins.append(final_gain.reshape(1, D))
        specs.append(_const_spec((1, D)))
    return pl.pallas_call(
        functools.partial(_ffn_kernel, which=which, fc=fc, final=final),
        out_shape=jax.ShapeDtypeStruct((B, L, D), F32),
        grid=(B, L // tm),
        in_specs=specs,
        out_specs=pl.BlockSpec((1, tm, D), lambda b, i: (b, i, 0)),
        compiler_params=_cparams(("parallel", "parallel")),
        name="ffn",
    )(*ins)


def _inproj_kernel(x_ref, mod_ref, g_ref, w_ref, *o_refs, splits):
    x = x_ref[0]
    h = _norm_mod(x, g_ref[...], mod_ref[0, 3:4, :], mod_ref[0, 4:5, :]).astype(BF16)
    off = 0
    for o_ref, n in zip(o_refs, splits):
        o_ref[0] = jnp.dot(h, w_ref[:, off:off + n], preferred_element_type=F32)
        off += n


def _inproj(x, mods, gain, w_in, splits):
    B, L, D = x.shape
    n = w_in.shape[1]
    assert sum(splits) == n
    tm = _time_tile(L, 512)
    return pl.pallas_call(
        functools.partial(_inproj_kernel, splits=splits),
        out_shape=[jax.ShapeDtypeStruct((B, L, s), F32) for s in splits],
        grid=(B, L // tm),
        in_specs=[pl.BlockSpec((1, tm, D), lambda b, i: (b, i, 0)),
                  pl.BlockSpec((1, N_MOD, D), lambda b, i: (b, 0, 0)),
                  _const_spec((1, D)), _const_spec((D, n))],
        out_specs=[pl.BlockSpec((1, tm, s), lambda b, i: (b, i, 0)) for s in splits],
        compiler_params=_cparams(("parallel", "parallel")),
        name="mixer_in_proj",
    )(x, mods, gain.reshape(1, D), w_in)


def _outproj_kernel(*refs, n_y):
    y_refs = refs[:n_y]
    x_ref, mod_ref, w_ref, o_ref = refs[n_y:]
    off = 0
    acc = None
    for y_ref in y_refs:
        k = y_ref.shape[2]
        part = jnp.dot(y_ref[0], w_ref[off:off + k, :], preferred_element_type=F32)
        acc = part if acc is None else acc + part
        off += k
    o_ref[0] = x_ref[0] + (1.0 + mod_ref[0, 5:6, :]) * acc


def _outproj(ys, x, mods, w_out):
    B, L, D = x.shape
    tm = _time_tile(L, 512)
    k = w_out.shape[0]
    assert sum(y.shape[2] for y in ys) == k
    return pl.pallas_call(
        functools.partial(_outproj_kernel, n_y=len(ys)),
        out_shape=jax.ShapeDtypeStruct((B, L, D), F32),
        grid=(B, L // tm),
        in_specs=[pl.BlockSpec((1, tm, y.shape[2]), lambda b, i: (b, i, 0)) for y in ys]
        + [pl.BlockSpec((1, tm, D), lambda b, i: (b, i, 0)),
           pl.BlockSpec((1, N_MOD, D), lambda b, i: (b, 0, 0)),
           _const_spec((k, D))],
        out_specs=pl.BlockSpec((1, tm, D), lambda b, i: (b, i, 0)),
        compiler_params=_cparams(("parallel", "parallel")),
        name="mixer_out_proj",
    )(*ys, x, mods, w_out)


def _s5_kernel(u_ref, h0_ref, perm_ref, permt_ref, wb_ref, wc_ref, a1r_ref, a1i_ref, apr_ref, api_ref,
               asr_ref, asi_ref, d_ref, wg_ref, y_ref, hT_ref, carry_ref, *, tseg):
    t = pl.program_id(1)

    @pl.when(t == 0)
    def _():
        carry_ref[...] = h0_ref[0]

    u = _dot3(perm_ref[...], u_ref[0])
    nblk = u.shape[1] // LANES
    half = wb_ref.shape[2] // 2
    rows8 = lax.broadcasted_iota(jnp.int32, (S5_SEGS, 1), 0)
    n_join = int(math.log2(S5_SEGS))
    ys = []
    for j in range(nblk):
        cols = slice(j * half, (j + 1) * half)
        bu = _bdot(u[:, j * LANES:(j + 1) * LANES], wb_ref[j])
        ar = a1r_ref[:, cols]
        ai = a1i_ref[:, cols]
        sr = jnp.zeros((S5_SEGS, half), F32)
        si = jnp.zeros((S5_SEGS, half), F32)
        xr_rows, xi_rows = [], []
        for i in range(tseg):
            br = bu[S5_SEGS * i:S5_SEGS * (i + 1), :half]
            bi = bu[S5_SEGS * i:S5_SEGS * (i + 1), half:]
            sr, si = ar * sr - ai * si + br, ar * si + ai * sr + bi
            xr_rows.append(sr)
            xi_rows.append(si)
        cr = jnp.where(rows8 == 0, carry_ref[:, 2 * j * half:(2 * j + 1) * half], pltpu.roll(sr, 1, axis=0))
        ci = jnp.where(rows8 == 0, carry_ref[:, (2 * j + 1) * half:(2 * j + 2) * half],
                       pltpu.roll(si, 1, axis=0))
        for m in range(n_join):
            s = 1 << m
            pr = asr_ref[S5_SEGS * m:S5_SEGS * (m + 1), cols]
            pi = asi_ref[S5_SEGS * m:S5_SEGS * (m + 1), cols]
            shr = pltpu.roll(cr, s, axis=0)
            shi = pltpu.roll(ci, s, axis=0)
            cr, ci = cr + pr * shr - pi * shi, ci + pr * shi + pi * shr
        for i in range(tseg):
            pr = apr_ref[S5_SEGS * i:S5_SEGS * (i + 1), cols]
            pi = api_ref[S5_SEGS * i:S5_SEGS * (i + 1), cols]
            xr_rows[i] = xr_rows[i] + pr * cr - pi * ci
            xi_rows[i] = xi_rows[i] + pr * ci + pi * cr
        carry_ref[:, 2 * j * half:(2 * j + 1) * half] = xr_rows[-1][S5_SEGS - 1:, :]
        carry_ref[:, (2 * j + 1) * half:(2 * j + 2) * half] = xi_rows[-1][S5_SEGS - 1:, :]
        xr = jnp.concatenate(xr_rows, axis=0)
        xi = jnp.concatenate(xi_rows, axis=0)
        ys.append(_bdot(xr, wc_ref[j, :half, :]) + _bdot(xi, wc_ref[j, half:, :]))
    y = jnp.concatenate(ys, axis=1) + u * d_ref[...]
    y = jax.nn.gelu(y)
    y = (y * _sigmoid(_bdot(y, wg_ref[...]))).astype(BF16)
    y_ref[0] = jnp.dot(permt_ref[...], y, preferred_element_type=F32).astype(y_ref.dtype)
    hT_ref[0] = carry_ref[...]


def _s5_tables(lam_re, lam_im, log_dt, b_re, b_im, c_re, c_im, tseg):
    G, P = lam_re.shape
    H = b_re.shape[2]
    gpb = LANES // H
    nblk = G // gpb
    dt = jnp.exp(log_dt)[:, None]
    mag = jnp.exp(lam_re * dt)
    ab_re = mag * jnp.cos(lam_im * dt)
    ab_im = mag * jnp.sin(lam_im * dt)
    den = lam_re * lam_re + lam_im * lam_im
    z_re = ((ab_re - 1.0) * lam_re + ab_im * lam_im) / den
    z_im = (ab_im * lam_re - (ab_re - 1.0) * lam_im) / den
    bb_re = z_re[..., None] * b_re - z_im[..., None] * b_im
    bb_im = z_re[..., None] * b_im + z_im[..., None] * b_re
    eye = jnp.eye(gpb, dtype=F32)

    def blockdiag_in(bb):
        bb = bb.reshape(nblk, gpb, P, H)
        m = jnp.einsum('jgph,gk->jghkp', bb, eye)
        return m.reshape(nblk, gpb * H, gpb * P)

    def blockdiag_out(cc):
        cc = cc.reshape(nblk, gpb, H, P)
        m = jnp.einsum('jghp,gk->jgpkh', cc, eye)
        return m.reshape(nblk, gpb * P, gpb * H)

    wb = jnp.concatenate([blockdiag_in(bb_re), blockdiag_in(bb_im)], axis=2).astype(BF16)
    wc = jnp.concatenate([blockdiag_out(c_re), blockdiag_out(-c_im)], axis=1).astype(BF16)

    def powers(n):
        n = n.astype(F32)[:, None, None]
        mg = jnp.exp(n * (lam_re * dt)[None])
        return ((mg * jnp.cos(n * (lam_im * dt)[None])).reshape(-1, G * P),
                (mg * jnp.sin(n * (lam_im * dt)[None])).reshape(-1, G * P))

    a1r, a1i = powers(jnp.ones((S5_SEGS,), jnp.int32))
    apr, api = powers(jnp.repeat(jnp.arange(tseg) + 1, S5_SEGS))
    n_join = int(math.log2(S5_SEGS))
    shifts = jnp.repeat(2 ** jnp.arange(n_join), S5_SEGS)
    asr, asi = powers(tseg * shifts)
    keep = (jnp.tile(jnp.arange(S5_SEGS), n_join) >= shifts)[:, None]
    return wb, wc, a1r, a1i, apr, api, jnp.where(keep, asr, 0.0), jnp.where(keep, asi, 0.0)


def _s5_state_to_cols(h_re, h_im, nblk):
    B, G, P = h_re.shape
    gpb = G // nblk
    st = jnp.stack([h_re.reshape(B, nblk, gpb * P), h_im.reshape(B, nblk, gpb * P)], axis=2)
    return st.reshape(B, 1, 2 * G * P)


def _s5_cols_to_state(cols, G, P, nblk):
    B = cols.shape[0]
    st = cols.reshape(B, nblk, 2, (G // nblk) * P)
    return st[:, :, 0].reshape(B, G, P), st[:, :, 1].reshape(B, G, P)


def _s5(u, h_re, h_im, lam_re, lam_im, log_dt, b_re, b_im, c_re, c_im, d_skip, w_glu_bf16):
    B, L, W = u.shape
    G, P = lam_re.shape
    T = _time_tile(L, 256)
    assert T % S5_SEGS == 0
    tseg = T // S5_SEGS
    nblk = W // LANES
    tables = _s5_tables(lam_re, lam_im, log_dt, b_re, b_im, c_re, c_im, tseg)
    r = jnp.arange(T)
    perm = ((r % S5_SEGS) * tseg + r // S5_SEGS)[:, None] == jnp.arange(T)[None, :]
    perm = perm.astype(BF16)
    h0 = _s5_state_to_cols(h_re, h_im, nblk)
    ncol = 2 * G * P
    consts = (_tile3(perm), perm.T) + tables + (d_skip.reshape(1, W), w_glu_bf16)
    y, hT = pl.pallas_call(
        functools.partial(_s5_kernel, tseg=tseg),
        out_shape=[jax.ShapeDtypeStruct((B, L, W), BF16), jax.ShapeDtypeStruct((B, 1, ncol), F32)],
        grid=(B, L // T),
        in_specs=[pl.BlockSpec((1, T, W), lambda b, t: (b, t, 0)),
                  pl.BlockSpec((1, 1, ncol), lambda b, t: (b, 0, 0))]
        + [_const_spec(c.shape) for c in consts],
        out_specs=[pl.BlockSpec((1, T, W), lambda b, t: (b, t, 0)),
                   pl.BlockSpec((1, 1, ncol), lambda b, t: (b, 0, 0))],
        scratch_shapes=[pltpu.VMEM((1, ncol), F32)],
        compiler_params=_cparams(("parallel", "arbitrary")),
        name="s5_mixer",
    )(u, h0, *consts)
    hr, hi = _s5_cols_to_state(hT[:, 0], G, P, nblk)
    return y, hr, hi


def _stack_heads(x, lane_lo):
    zero = jnp.zeros_like(x)
    return jnp.concatenate([jnp.where(lane_lo, x, zero), jnp.where(lane_lo, zero, x)], axis=0)


def _head_sum(x, hm2):
    nblk = x.shape[1] // LANES
    R = x.shape[0]
    xs = jnp.concatenate([x[:, j * LANES:(j + 1) * LANES] for j in range(nblk)], axis=0)
    s = _dot2(xs, hm2)
    return jnp.concatenate([s[j * R:(j + 1) * R] for j in range(nblk)], axis=1)


def _rwkv_kernel(p_ref, sp_ref, s0_ref, mu_ref, w0_ref, ww2_ref, a0_ref, wa2_ref, wg2_ref, kk_ref, ka_ref,
                 rk_ref, lng_ref, lnb_ref, hm_ref, tri_ref, y_ref, sh_ref, sT_ref, prev_ref, st_ref, *, width):
    t = pl.program_id(1)
    W = width
    NB, C, _ = p_ref.shape

    @pl.when(t == 0)
    def _():
        prev_ref[...] = sp_ref[...]
        st_ref[...] = s0_ref[...]

    rows = lax.broadcasted_iota(jnp.int32, (C, 1), 0)
    ps = []
    for n in range(NB):
        p = p_ref[n]
        p_prev = jnp.where(rows == 0, prev_ref[n], pltpu.roll(p, 1, axis=0))
        prev_ref[n] = p[C - 1:C, :]
        sh_ref[n] = p[C - 1:C, :]
        ps.append(p + (p_prev - p) * mu_ref[...])
    ps = jnp.concatenate(ps, axis=0)
    r = ps[:, :W]
    k = ps[:, W:2 * W]
    v = ps[:, 2 * W:3 * W]
    wa_low = ps[:, 3 * W:3 * W + LANES]
    g_low = ps[:, 3 * W + LANES:]
    xw = w0_ref[...] + _bdot(jnp.tanh(wa_low), ww2_ref[...])
    lw = -jnp.exp(-_softplus(-xw) - 0.5)
    a = _sigmoid(a0_ref[...] + _bdot(wa_low, wa2_ref[...]))
    g = _bdot(_sigmoid(g_low), wg2_ref[...])
    hm = hm_ref[...]
    kk = k * kk_ref[...]
    kk = kk / jnp.maximum(jnp.sqrt(_head_sum(kk * kk, hm)), 1e-12)
    kmod = k * (1.0 + (a - 1.0) * ka_ref[...])
    bvec = kk * a

    cum = _dot3(tri_ref[...], lw)
    e_in = jnp.exp(cum)
    e_ex = jnp.exp(cum - lw)
    e_neg = jnp.exp(-cum)
    cum_last = [cum[(n + 1) * C - 1:(n + 1) * C, :] for n in range(NB)]
    e_end = jnp.exp(jnp.concatenate([jnp.broadcast_to(c, (C, W)) for c in cum_last], axis=0) - cum)
    w_end = [jnp.exp(c) for c in cum_last]
    at = (-kk * e_ex).astype(BF16)
    rt = (r * e_in).astype(BF16)
    bt = (bvec * e_neg).astype(BF16)
    kt = (kmod * e_neg).astype(BF16)
    bw = (bvec * e_end).astype(BF16)
    kw = (kmod * e_end).astype(BF16)
    vb = v.astype(BF16)

    C2 = 2 * C
    si = lax.broadcasted_iota(jnp.int32, (C, C2), 0)
    sj = lax.broadcasted_iota(jnp.int32, (C, C2), 1) % C
    strict = si > sj
    lower = si >= sj
    lane_lo = lax.broadcasted_iota(jnp.int32, (1, LANES), 1) < RWKV_HEAD
    side_lo = lax.broadcasted_iota(jnp.int32, (1, C2), 1) < C
    npair = W // LANES
    chains = [(n, j) for n in range(NB) for j in range(npair)]
    x0, vst, bkw, l_ab, a_ak, a_r = {}, {}, {}, {}, {}, {}
    for c in chains:
        n, j = c
        pair = lambda x: x[n * C:(n + 1) * C, j * LANES:(j + 1) * LANES]
        ar = jnp.concatenate([pair(at), pair(rt)], axis=0)
        bk_st = jnp.concatenate([_stack_heads(pair(bt), lane_lo), _stack_heads(pair(kt), lane_lo)], axis=0)
        m = _bdot_nt(ar, bk_st)
        x0[c] = _bdot_nt(ar, st_ref[n, j])
        vst[c] = _stack_heads(pair(vb), lane_lo)
        bkw[c] = jnp.concatenate([_stack_heads(pair(bw), lane_lo), _stack_heads(pair(kw), lane_lo)], axis=0)
        l_ab[c] = jnp.where(strict, m[:C, :C2], 0.0)
        a_ak[c] = jnp.where(strict, m[:C, C2:], 0.0).astype(BF16)
        a_r[c] = jnp.concatenate([jnp.where(lower, m[C:, :C2], 0.0), jnp.where(lower, m[C:, C2:], 0.0)],
                                 axis=1).astype(BF16)
    def split(x):
        hi = x.astype(BF16)
        return hi, (x - hi.astype(F32)).astype(BF16)

    def lhs3(x):
        hi, lo = split(x)
        return jnp.concatenate([hi, lo, hi], axis=1)

    def rhs3(x, mask):
        hi, lo = split(x)
        hi = _stack_heads(hi, mask)
        return jnp.concatenate([hi, hi, _stack_heads(lo, mask)], axis=0)

    blk_t = si // RWKV_BASE
    blk_s = sj // RWKV_BASE
    diag8 = blk_t == blk_s
    tinv = {c: jnp.where(si == sj, 1.0, 0.0) + jnp.where(diag8, l_ab[c], 0.0) for c in chains}
    pw = {c: jnp.where(diag8, l_ab[c], 0.0) for c in chains}
    for _ in range(int(math.log2(RWKV_BASE)) - 1):
        for c in chains:
            pw[c] = _bdot(pw[c], _stack_heads(pw[c].astype(BF16), side_lo))
        for c in chains:
            tinv[c] = tinv[c] + _bdot(pw[c], _stack_heads(tinv[c].astype(BF16), side_lo))
    b = RWKV_BASE
    while b < C:
        lower_left = ((si // (2 * b)) == (sj // (2 * b))) & (((si // b) % 2) == 1) & (((sj // b) % 2) == 0)
        lt = {c: jnp.dot(lhs3(jnp.where(lower_left, l_ab[c], 0.0)), rhs3(tinv[c], side_lo),
                         preferred_element_type=F32) for c in chains}
        for c in chains:
            tinv[c] = tinv[c] + jnp.dot(lhs3(tinv[c]), rhs3(lt[c], side_lo), preferred_element_type=F32)
        b *= 2
    z = {c: x0[c][:C] + jnp.dot(a_ak[c], vst[c], preferred_element_type=F32) for c in chains}
    u = {c: jnp.dot(lhs3(tinv[c]), rhs3(z[c], lane_lo), preferred_element_type=F32) for c in chains}
    uv = {c: jnp.concatenate([_stack_heads(u[c].astype(BF16), lane_lo), vst[c]], axis=0) for c in chains}
    y_p = {c: x0[c][C:] + jnp.dot(a_r[c], uv[c], preferred_element_type=F32) for c in chains}
    for c in chains:
        n, j = c
        st_ref[n, j] = st_ref[n, j] * w_end[n][:, j * LANES:(j + 1) * LANES] + _bdot_tn(uv[c], bkw[c])
    y = jnp.concatenate([jnp.concatenate([y_p[(n, j)] for j in range(npair)], axis=1)
                         for n in range(NB)], axis=0)
    inv_n = 1.0 / RWKV_HEAD
    mean = _head_sum(y, hm) * inv_n
    yc = y - mean
    var = _head_sum(yc * yc, hm) * inv_n
    yn = yc * lax.rsqrt(var + RWKV_GN_EPS) * lng_ref[...] + lnb_ref[...]
    bonus = _head_sum(r * kmod * rk_ref[...], hm) * v
    out = ((yn + bonus) * g).astype(y_ref.dtype)
    for n in range(NB):
        y_ref[n] = out[n * C:(n + 1) * C]

    @pl.when(t == pl.num_programs(1) - 1)
    def _():
        sT_ref[...] = st_ref[...]


def _rwkv(p, shift_prev, s0, mu, w0, w_w2, a0, w_a2, w_g2, k_k, k_a, r_k, lnx_g, lnx_b):
    B, L, cols = p.shape
    H, N = s0.shape[1], s0.shape[2]
    W = H * N
    C = _time_tile(L, 64)
    assert 1 << int(math.log2(C)) == C and 2 * N == LANES
    npair = H // 2
    s0p = s0.reshape(B, npair, 2, N, N)
    z = jnp.zeros_like(s0p[:, :, 0])
    s0bd = jnp.concatenate([jnp.concatenate([s0p[:, :, 0], z], axis=-1),
                            jnp.concatenate([z, s0p[:, :, 1]], axis=-1)], axis=-2)
    lo = w_w2.shape[0]
    ww2p = jnp.concatenate([w_w2, jnp.zeros((LANES - lo, W), F32)], axis=0).astype(BF16)
    wa2p = jnp.concatenate([jnp.zeros((lo, W), F32), w_a2], axis=0).astype(BF16)
    hid = jnp.arange(LANES) // N
    hm = (hid[:, None] == hid[None, :]).astype(BF16)
    hm2 = jnp.concatenate([hm, hm], axis=0)
    nb = RWKV_SEQS if B % RWKV_SEQS == 0 else 1
    tr = jnp.arange(nb * C)
    tri = ((tr[:, None] >= tr[None, :]) & (tr[:, None] // C == tr[None, :] // C)).astype(BF16)
    row = lambda x: x.reshape(1, -1)
    y, sh, sT = pl.pallas_call(
        functools.partial(_rwkv_kernel, width=W),
        out_shape=[jax.ShapeDtypeStruct((B, L, W), BF16),
                   jax.ShapeDtypeStruct((B, 1, cols), F32),
                   jax.ShapeDtypeStruct((B, npair, LANES, LANES), F32)],
        grid=(B // nb, L // C),
        in_specs=[pl.BlockSpec((nb, C, cols), lambda b, t: (b, t, 0)),
                  pl.BlockSpec((nb, 1, cols), lambda b, t: (b, 0, 0)),
                  pl.BlockSpec((nb, npair, LANES, LANES), lambda b, t: (b, 0, 0, 0)),
                  _const_spec((1, cols)), _const_spec((1, W)), _const_spec((LANES, W)), _const_spec((1, W)),
                  _const_spec((LANES, W)), _const_spec((LANES, W)), _const_spec((1, W)), _const_spec((1, W)),
                  _const_spec((1, W)), _const_spec((1, W)), _const_spec((1, W)), _const_spec((2 * LANES, LANES)),
                  _const_spec((nb * C, 3 * nb * C))],
        out_specs=[pl.BlockSpec((nb, C, W), lambda b, t: (b, t, 0)),
                   pl.BlockSpec((nb, 1, cols), lambda b, t: (b, 0, 0)),
                   pl.BlockSpec((nb, npair, LANES, LANES), lambda b, t: (b, 0, 0, 0))],
        scratch_shapes=[pltpu.VMEM((nb, 1, cols), F32), pltpu.VMEM((nb, npair, LANES, LANES), F32)],
        compiler_params=_cparams(("parallel", "arbitrary")),
        name="rwkv7_mixer",
    )(p, shift_prev.reshape(B, 1, cols), s0bd, row(mu), row(w0), ww2p, row(a0), wa2p, w_g2.astype(BF16),
      row(k_k), row(k_a), row(r_k), row(lnx_g), row(lnx_b), hm2, _tile3(tri))
    sT = sT.reshape(B, npair, 2, N, 2, N)
    s_new = jnp.stack([sT[:, :, 0, :, 0, :], sT[:, :, 1, :, 1, :]], axis=2).reshape(B, H, N, N)
    return y, sh[:, 0], s_new


def _hgrn_segment_matrix(C):
    t = jnp.arange(C)[:, None]
    r = jnp.arange(C)[None, :]
    blocks = [r <= t]
    h = C // 2
    while h >= 1:
        mid = (t // (2 * h)) * (2 * h) + h - 1
        is_q = (t & h) != 0
        blocks.append(jnp.where(is_q, (r > mid) & (r <= t), (r > t) & (r <= mid)))
        h //= 2
    return jnp.concatenate(blocks, axis=0).astype(BF16)


def _hgrn_kernel(p_ref, s0_ref, llb_ref, l1lb_ref, omlb_ref, gn_ref, mseg_ref, y_ref, sT_ref, st_ref, *, width):
    t = pl.program_id(1)
    W = width
    H = W // HGRN_HEAD
    NB, C, _ = p_ref.shape

    @pl.when(t == 0)
    def _():
        for n in range(NB):
            for h in range(H):
                st_ref[n, h] = s0_ref[n, h].T

    p = jnp.concatenate([p_ref[n] for n in range(NB)], axis=0)
    q = _silu(p[:, :W])
    f = p[:, W:2 * W]
    v = p[:, 2 * W:3 * W]
    g = p[:, 3 * W:]
    ef = jnp.exp(-jnp.abs(f))
    lsig = jnp.minimum(f, 0.0) - jnp.log(1.0 + ef)
    c1 = llb_ref[...]
    c2 = l1lb_ref[...] + lsig
    log_f = jnp.maximum(c1, c2) + jnp.log(1.0 + jnp.exp(-jnp.abs(c1 - c2)))
    kv = omlb_ref[...] * (jnp.where(f >= 0.0, ef, 1.0) / (1.0 + ef))
    d_all = [_dot3(mseg_ref[...], log_f[n * C:(n + 1) * C]) for n in range(NB)]
    b = jnp.concatenate([d[:C] for d in d_all], axis=0)
    b_last = [d[C - 1:C, :] for d in d_all]
    qe = (q * jnp.exp(b)).astype(BF16)
    ke = (kv * jnp.exp(jnp.concatenate([jnp.broadcast_to(bl, (C, W)) for bl in b_last], axis=0) - b)).astype(BF16)
    e_last = [jnp.exp(bl) for bl in b_last]
    qk = q * kv
    vb = v.astype(BF16)
    rowi = lax.broadcasted_iota(jnp.int32, (NB * C, 1), 0)
    ti = lax.broadcasted_iota(jnp.int32, (C, C), 0)
    tj = lax.broadcasted_iota(jnp.int32, (C, C), 1)
    zs, masks = [], []
    hsz = C // 2
    lvl = 1
    while hsz >= 1:
        is_q = (rowi & hsz) != 0
        seg = jnp.concatenate([d[lvl * C:(lvl + 1) * C] for d in d_all], axis=0)
        zs.append((jnp.exp(seg) * jnp.where(is_q, q, kv)).astype(BF16))
        masks.append(((ti & hsz) != 0) & ((tj & hsz) == 0) & ((ti // (2 * hsz)) == (tj // (2 * hsz))))
        hsz //= 2
        lvl += 1
    chains = [(n, h) for n in range(NB) for h in range(H)]
    blk = lambda x, c: x[c[0] * C:(c[0] + 1) * C, c[1] * HGRN_HEAD:(c[1] + 1) * HGRN_HEAD]
    att = {}
    for z, mask in zip(zs, masks):
        for c in chains:
            zh = blk(z, c)
            term = jnp.where(mask, _bdot_nt(zh, zh), 0.0)
            att[c] = term if c not in att else att[c] + term
    o = {c: (_bdot(att[c], blk(vb, c)) + _bdot_nt(blk(qe, c), st_ref[c[0], c[1]])
             + jnp.sum(blk(qk, c), axis=-1, keepdims=True) * blk(v, c)) for c in chains}
    for c in chains:
        n, h = c
        st_ref[n, h] = (st_ref[n, h] * e_last[n][:, h * HGRN_HEAD:(h + 1) * HGRN_HEAD]
                        + _bdot_tn(blk(vb, c), blk(ke, c)))
    for c in chains:
        o[c] = o[c] * lax.rsqrt(jnp.mean(o[c] * o[c], axis=-1, keepdims=True) + NORM_EPS) * gn_ref[...]
    o = jnp.concatenate([jnp.concatenate([o[(n, h)] for h in range(H)], axis=1) for n in range(NB)], axis=0)
    out = (o * _silu(g)).astype(y_ref.dtype)
    for n in range(NB):
        y_ref[n] = out[n * C:(n + 1) * C]

    @pl.when(t == pl.num_programs(1) - 1)
    def _():
        for n in range(NB):
            for h in range(H):
                sT_ref[n, h] = st_ref[n, h].T


def _hgrn(p, s0, lb, gnorm):
    B, L, cols = p.shape
    W = cols // 4
    H = W // HGRN_HEAD
    C = _time_tile(L, 64)
    assert 1 << int(math.log2(C)) == C
    mseg = _tile3(_hgrn_segment_matrix(C))
    nb = HGRN_SEQS if B % HGRN_SEQS == 0 else 1
    row = lambda x: x.reshape(1, -1)
    y, sT = pl.pallas_call(
        functools.partial(_hgrn_kernel, width=W),
        out_shape=[jax.ShapeDtypeStruct((B, L, W), BF16),
                   jax.ShapeDtypeStruct((B, H, HGRN_HEAD, HGRN_HEAD), F32)],
        grid=(B // nb, L // C),
        in_specs=[pl.BlockSpec((nb, C, cols), lambda b, t: (b, t, 0)),
                  pl.BlockSpec((nb, H, HGRN_HEAD, HGRN_HEAD), lambda b, t: (b, 0, 0, 0)),
                  _const_spec((1, W)), _const_spec((1, W)), _const_spec((1, W)), _const_spec((1, HGRN_HEAD)),
                  _const_spec(mseg.shape)],
        out_specs=[pl.BlockSpec((nb, C, W), lambda b, t: (b, t, 0)),
                   pl.BlockSpec((nb, H, HGRN_HEAD, HGRN_HEAD), lambda b, t: (b, 0, 0, 0))],
        scratch_shapes=[pltpu.VMEM((nb, H, HGRN_HEAD, HGRN_HEAD), F32)],
        compiler_params=_cparams(("parallel", "arbitrary")),
        name="hgrn2_mixer",
    )(p, s0, row(jnp.log(lb)), row(jnp.log1p(-lb)), row(1.0 - lb), row(gnorm), mseg)
    return y, sT


def _trunk(x, mods_all, s5_re, s5_im, rwkv_s, rwkv_shift, hgrn_s, P, Pb):
    depth = P['norm_mix'].shape[0]
    lb_all = jax.nn.softmax(P['hgrn_lower_bounds'], axis=0)
    lb_all = jnp.cumsum(lb_all, axis=0) - lb_all[0]
    s5_width = P['s5_w_glu'].shape[1]
    n_re, n_im, n_rw, n_sh, n_hg = [], [], [], [], []
    for l in range(depth):
        mods = mods_all[l]
        x = _ffn(x, mods, P['norm_ffn1'][l], Pb['ffn1_w1'][l], Pb['ffn1_w3'][l], Pb['ffn1_w2'][l], 0)
        i = l // 2
        if l % 2 == 0:
            n_in = Pb['ab_w_in'].shape[2]
            u, pr = _inproj(x, mods, P['norm_mix'][l], Pb['ab_w_in'][i], (s5_width, n_in - s5_width))
            ya, hr, hi = _s5(u, s5_re[i], s5_im[i], P['s5_lam_re'][i], P['s5_lam_im'][i], P['s5_log_dt'][i],
                             P['s5_b_re'][i], P['s5_b_im'][i], P['s5_c_re'][i], P['s5_c_im'][i],
                             P['s5_d'][i], Pb['s5_w_glu'][i])
            yb, sh_new, rw_new = _rwkv(pr, rwkv_shift[i], rwkv_s[i], P['rwkv_mu'][i], P['rwkv_w0'][i],
                                       P['rwkv_w_w2'][i], P['rwkv_a0'][i], P['rwkv_w_a2'][i],
                                       P['rwkv_w_g2'][i], P['rwkv_k_k'][i], P['rwkv_k_a'][i],
                                       P['rwkv_r_k'][i], P['rwkv_lnx_g'][i], P['rwkv_lnx_b'][i])
            x = _outproj([ya, yb], x, mods, Pb['ab_w_out'][i])
            n_re.append(hr)
            n_im.append(hi)
            n_rw.append(rw_new)
            n_sh.append(sh_new)
        else:
            (pc,) = _inproj(x, mods, P['norm_mix'][l], Pb['c_w_in'][i], (Pb['c_w_in'].shape[2],))
            yc, hg_new = _hgrn(pc, hgrn_s[i], lb_all[l], P['hgrn_gnorm'][i])
            x = _outproj([yc], x, mods, Pb['c_w_out'][i])
            n_hg.append(hg_new)
        fin = P['final_norm'] if l == depth - 1 else None
        x = _ffn(x, mods, P['norm_ffn2'][l], Pb['ffn2_w1'][l], Pb['ffn2_w3'][l], Pb['ffn2_w2'][l], 2,
                 final_gain=fin)
    return (x, jnp.stack(n_re), jnp.stack(n_im), jnp.stack(n_rw), jnp.stack(n_sh), jnp.stack(n_hg))


def kernel(x_prompt, x_sample, state_s5_re, state_s5_im, state_rwkv, state_rwkv_shift, state_hgrn, c_prompt, c_sample, w_ada, b_ada, norm_ffn1, norm_mix, norm_ffn2, ffn1_w1, ffn1_w3, ffn1_w2, ffn2_w1, ffn2_w3, ffn2_w2, ab_w_in, ab_w_out, s5_lam_re, s5_lam_im, s5_log_dt, s5_b_re, s5_b_im, s5_c_re, s5_c_im, s5_d, s5_w_glu, rwkv_mu, rwkv_w0, rwkv_w_w2, rwkv_a0, rwkv_w_a2, rwkv_w_g2, rwkv_k_k, rwkv_k_a, rwkv_r_k, rwkv_lnx_g, rwkv_lnx_b, c_w_in, c_w_out, hgrn_lower_bounds, hgrn_gnorm, final_norm):
    P = dict(norm_ffn1=norm_ffn1, norm_mix=norm_mix, norm_ffn2=norm_ffn2,
             s5_lam_re=s5_lam_re, s5_lam_im=s5_lam_im, s5_log_dt=s5_log_dt, s5_b_re=s5_b_re, s5_b_im=s5_b_im,
             s5_c_re=s5_c_re, s5_c_im=s5_c_im, s5_d=s5_d, s5_w_glu=s5_w_glu, rwkv_mu=rwkv_mu,
             rwkv_w0=rwkv_w0, rwkv_w_w2=rwkv_w_w2, rwkv_a0=rwkv_a0, rwkv_w_a2=rwkv_w_a2, rwkv_w_g2=rwkv_w_g2,
             rwkv_k_k=rwkv_k_k, rwkv_k_a=rwkv_k_a, rwkv_r_k=rwkv_r_k, rwkv_lnx_g=rwkv_lnx_g,
             rwkv_lnx_b=rwkv_lnx_b, hgrn_lower_bounds=hgrn_lower_bounds, hgrn_gnorm=hgrn_gnorm,
             final_norm=final_norm)
    Pb = {n: w.astype(BF16) for n, w in dict(
        ffn1_w1=ffn1_w1, ffn1_w3=ffn1_w3, ffn1_w2=ffn1_w2, ffn2_w1=ffn2_w1, ffn2_w3=ffn2_w3, ffn2_w2=ffn2_w2,
        ab_w_in=ab_w_in, ab_w_out=ab_w_out, c_w_in=c_w_in, c_w_out=c_w_out, s5_w_glu=s5_w_glu).items()}
    depth, d = norm_mix.shape
    bp, bs = x_prompt.shape[0], x_sample.shape[0]
    mods = _ada(jnp.concatenate([c_prompt, c_sample], axis=0), w_ada, b_ada)
    mods = mods.reshape(depth, bp + bs, N_MOD, d)
    dt = x_prompt.dtype
    zeros_like_b = lambda s: jnp.zeros((s.shape[0], bp) + s.shape[2:], dt)
    outs_p = _trunk(x_prompt, mods[:, :bp], zeros_like_b(state_s5_re), zeros_like_b(state_s5_im),
                    zeros_like_b(state_rwkv), zeros_like_b(state_rwkv_shift), zeros_like_b(state_hgrn), P, Pb)
    outs_s = _trunk(x_sample, mods[:, bp:], state_s5_re, state_s5_im, state_rwkv, state_rwkv_shift,
                    state_hgrn, P, Pb)
    return (outs_p[0], outs_s[0]) + tuple(outs_p[1:]) + tuple(outs_s[1:])
```

```python
import functools
import math

import jax
import jax.numpy as jnp
from jax import lax
from jax.experimental import pallas as pl
from jax.experimental.pallas import tpu as pltpu

F32 = jnp.float32
BF16 = jnp.bfloat16

NORM_EPS = 1e-6
RWKV_GN_EPS = 64e-5
N_MOD = 9
RWKV_HEAD = 64
RWKV_BASE = 8
HGRN_HEAD = 128
LANES = 128
S5_SEGS = 8
RWKV_SEQS = 4
HGRN_SEQS = 4
VMEM_LIMIT = 56 * 1024 * 1024
FLAT_BELOW = 64


def _bdot(a, b):
    return jnp.dot(a.astype(BF16), b.astype(BF16), preferred_element_type=F32)


def _bdot_nt(a, b):
    return lax.dot_general(a.astype(BF16), b.astype(BF16), (((1,), (1,)), ((), ())),
                           preferred_element_type=F32)


def _bdot_tn(a, b):
    return lax.dot_general(a.astype(BF16), b.astype(BF16), (((0,), (0,)), ((), ())),
                           preferred_element_type=F32)


def _dot2(a, b2):
    hi = a.astype(BF16)
    lo = (a - hi.astype(F32)).astype(BF16)
    return jnp.dot(jnp.concatenate([hi, lo], axis=1), b2, preferred_element_type=F32)


def _dot3(m3, x):
    hi = x.astype(BF16)
    r1 = x - hi.astype(F32)
    mid = r1.astype(BF16)
    lo = (r1 - mid.astype(F32)).astype(BF16)
    return jnp.dot(m3, jnp.concatenate([hi, mid, lo], axis=0), preferred_element_type=F32)


def _tile3(m):
    return jnp.concatenate([m, m, m], axis=1)


def _sigmoid(x):
    return 1.0 / (1.0 + jnp.exp(-x))


def _silu(x):
    return x * _sigmoid(x)


def _softplus(x):
    return jnp.maximum(x, 0.0) + jnp.log(1.0 + jnp.exp(-jnp.abs(x)))


def _cparams(sem):
    return pltpu.CompilerParams(dimension_semantics=sem, vmem_limit_bytes=VMEM_LIMIT)


def _const_spec(shape):
    nd = len(shape)
    return pl.BlockSpec(shape, lambda *_: (0,) * nd)


def _time_tile(L, want):
    t = min(L, want)
    assert L % t == 0 and t % 8 == 0
    return t


def _ada_kernel(c_ref, w_ref, b_ref, o_ref):
    cs = _silu(c_ref[...])
    o_ref[0] = _bdot(cs, w_ref[0]) + b_ref[0]


def _ada(c_all, w_ada, b_ada):
    depth, d, n = w_ada.shape
    nb = c_all.shape[0]
    tn = 1024
    return pl.pallas_call(
        _ada_kernel,
        out_shape=jax.ShapeDtypeStruct((depth, nb, n), F32),
        grid=(depth, n // tn),
        in_specs=[pl.BlockSpec((nb, d), lambda l, j: (0, 0)),
                  pl.BlockSpec((1, d, tn), lambda l, j: (l, 0, j)),
                  pl.BlockSpec((1, 1, tn), lambda l, j: (l, 0, j))],
        out_specs=pl.BlockSpec((1, nb, tn), lambda l, j: (l, 0, j)),
        compiler_params=_cparams(("arbitrary", "arbitrary")),
        name="ada_mod",
    )(c_all, w_ada, b_ada.reshape(depth, 1, n))


def _norm_mod(x, gain, shift, scale):
    y = x * lax.rsqrt(jnp.mean(x * x, axis=-1, keepdims=True) + NORM_EPS) * gain
    return y * (1.0 + scale) + shift


def _ffn_kernel(x_ref, mod_ref, g_ref, w1_ref, w3_ref, w2_ref, *rest, which, fc, final, n_y):
    y_refs = rest[:n_y]
    rest = rest[n_y:]
    if n_y:
        wo_ref, rest = rest[0], rest[1:]
    if final:
        gf_ref, o_ref = rest
    else:
        (o_ref,) = rest
    x = x_ref[0]
    if n_y:
        off = 0
        mix = None
        for y_ref in y_refs:
            k = y_ref.shape[2]
            part = jnp.dot(y_ref[0], wo_ref[off:off + k, :], preferred_element_type=F32)
            mix = part if mix is None else mix + part
            off += k
        x = x + (1.0 + mod_ref[0, 5]) * mix
    shift = mod_ref[0, 3 * which]
    scale = mod_ref[0, 3 * which + 1]
    gate = mod_ref[0, 3 * which + 2]
    h = _norm_mod(x, g_ref[...], shift, scale).astype(BF16)
    d_ff = w1_ref.shape[1]
    acc = jnp.zeros(x.shape, F32)
    for c in range(d_ff // fc):
        a = jnp.dot(h, w1_ref[:, c * fc:(c + 1) * fc], preferred_element_type=F32)
        b = jnp.dot(h, w3_ref[:, c * fc:(c + 1) * fc], preferred_element_type=F32)
        u = (_silu(a) * b).astype(BF16)
        acc = acc + jnp.dot(u, w2_ref[c * fc:(c + 1) * fc, :], preferred_element_type=F32)
    y = x + 0.5 * (1.0 + gate) * acc
    if final:
        y = y * lax.rsqrt(jnp.mean(y * y, axis=-1, keepdims=True) + NORM_EPS) * gf_ref[...]
    o_ref[0] = y


def _mod_spec(mods, tm):
    _, n_mod, R, D = mods.shape
    if R == 1:
        return pl.BlockSpec((1, n_mod, 1, D), lambda b, i: (b, 0, 0, 0))
    return pl.BlockSpec((1, n_mod, tm, D), lambda b, i: (b, 0, i, 0))


def _ffn(x, mods, gain, w1, w3, w2, which, final_gain=None, ys=(), w_out=None):
    B, L, D = x.shape
    d_ff = w1.shape[1]
    tm = _time_tile(L, 512)
    fc = 256
    assert d_ff % fc == 0
    final = final_gain is not None
    ins = [x, mods, gain.reshape(1, D), w1, w3, w2]
    specs = [pl.BlockSpec((1, tm, D), lambda b, i: (b, i, 0)), _mod_spec(mods, tm),
             _const_spec((1, D)), _const_spec((D, d_ff)), _const_spec((D, d_ff)), _const_spec((d_ff, D))]
    if ys:
        assert sum(y.shape[2] for y in ys) == w_out.shape[0]
        ins += list(ys) + [w_out]
        specs += [pl.BlockSpec((1, tm, y.shape[2]), lambda b, i: (b, i, 0)) for y in ys]
        specs.append(_const_spec(w_out.shape))
    if final:
        ins.append(final_gain.reshape(1, D))
        specs.append(_const_spec((1, D)))
    return pl.pallas_call(
        functools.partial(_ffn_kernel, which=which, fc=fc, final=final, n_y=len(ys)),
        out_shape=jax.ShapeDtypeStruct((B, L, D), F32),
        grid=(B, L // tm),
        in_specs=specs,
        out_specs=pl.BlockSpec((1, tm, D), lambda b, i: (b, i, 0)),
        compiler_params=_cparams(("parallel", "parallel")),
        name="ffn",
    )(*ins)


def _inproj_kernel(x_ref, mod_ref, g_ref, w_ref, *o_refs, splits):
    x = x_ref[0]
    h = _norm_mod(x, g_ref[...], mod_ref[0, 3], mod_ref[0, 4]).astype(BF16)
    off = 0
    for o_ref, n in zip(o_refs, splits):
        o_ref[0] = jnp.dot(h, w_ref[:, off:off + n], preferred_element_type=F32)
        off += n


def _inproj(x, mods, gain, w_in, splits):
    B, L, D = x.shape
    n = w_in.shape[1]
    assert sum(splits) == n
    tm = _time_tile(L, 512)
    return pl.pallas_call(
        functools.partial(_inproj_kernel, splits=splits),
        out_shape=[jax.ShapeDtypeStruct((B, L, s), F32) for s in splits],
        grid=(B, L // tm),
        in_specs=[pl.BlockSpec((1, tm, D), lambda b, i: (b, i, 0)), _mod_spec(mods, tm),
                  _const_spec((1, D)), _const_spec((D, n))],
        out_specs=[pl.BlockSpec((1, tm, s), lambda b, i: (b, i, 0)) for s in splits],
        compiler_params=_cparams(("parallel", "parallel")),
        name="mixer_in_proj",
    )(x, mods, gain.reshape(1, D), w_in)


def _s5_kernel(u_ref, h0_ref, perm_ref, permt_ref, wb_ref, wc_ref, a1r_ref, a1i_ref, apr_ref, api_ref,
               asr_ref, asi_ref, d_ref, wg_ref, y_ref, hT_ref, carry_ref, *, tseg):
    t = pl.program_id(1)

    @pl.when(t == 0)
    def _():
        carry_ref[...] = h0_ref[0]

    u = _dot3(perm_ref[...], u_ref[0])
    nblk = u.shape[1] // LANES
    half = wb_ref.shape[2] // 2
    rows8 = lax.broadcasted_iota(jnp.int32, (S5_SEGS, 1), 0)
    n_join = int(math.log2(S5_SEGS))
    ys = []
    for j in range(nblk):
        cols = slice(j * half, (j + 1) * half)
        bu = _bdot(u[:, j * LANES:(j + 1) * LANES], wb_ref[j])
        ar = a1r_ref[:, cols]
        ai = a1i_ref[:, cols]
        sr = jnp.zeros((S5_SEGS, half), F32)
        si = jnp.zeros((S5_SEGS, half), F32)
        xr_rows, xi_rows = [], []
        for i in range(tseg):
            br = bu[S5_SEGS * i:S5_SEGS * (i + 1), :half]
            bi = bu[S5_SEGS * i:S5_SEGS * (i + 1), half:]
            sr, si = ar * sr - ai * si + br, ar * si + ai * sr + bi
            xr_rows.append(sr)
            xi_rows.append(si)
        cr = jnp.where(rows8 == 0, carry_ref[:, 2 * j * half:(2 * j + 1) * half], pltpu.roll(sr, 1, axis=0))
        ci = jnp.where(rows8 == 0, carry_ref[:, (2 * j + 1) * half:(2 * j + 2) * half],
                       pltpu.roll(si, 1, axis=0))
        for m in range(n_join):
            s = 1 << m
            pr = asr_ref[S5_SEGS * m:S5_SEGS * (m + 1), cols]
            pi = asi_ref[S5_SEGS * m:S5_SEGS * (m + 1), cols]
            shr = pltpu.roll(cr, s, axis=0)
            shi = pltpu.roll(ci, s, axis=0)
            cr, ci = cr + pr * shr - pi * shi, ci + pr * shi + pi * shr
        for i in range(tseg):
            pr = apr_ref[S5_SEGS * i:S5_SEGS * (i + 1), cols]
            pi = api_ref[S5_SEGS * i:S5_SEGS * (i + 1), cols]
            xr_rows[i] = xr_rows[i] + pr * cr - pi * ci
            xi_rows[i] = xi_rows[i] + pr * ci + pi * cr
        carry_ref[:, 2 * j * half:(2 * j + 1) * half] = xr_rows[-1][S5_SEGS - 1:, :]
        carry_ref[:, (2 * j + 1) * half:(2 * j + 2) * half] = xi_rows[-1][S5_SEGS - 1:, :]
        xr = jnp.concatenate(xr_rows, axis=0)
        xi = jnp.concatenate(xi_rows, axis=0)
        ys.append(_bdot(xr, wc_ref[j, :half, :]) + _bdot(xi, wc_ref[j, half:, :]))
    y = jnp.concatenate(ys, axis=1) + u * d_ref[...]
    y = jax.nn.gelu(y)
    y = (y * _sigmoid(_bdot(y, wg_ref[...]))).astype(BF16)
    y_ref[0] = jnp.dot(permt_ref[...], y, preferred_element_type=F32).astype(y_ref.dtype)
    hT_ref[0] = carry_ref[...]


def _s5_tables(lam_re, lam_im, log_dt, b_re, b_im, c_re, c_im, tseg):
    G, P = lam_re.shape
    H = b_re.shape[2]
    gpb = LANES // H
    nblk = G // gpb
    dt = jnp.exp(log_dt)[:, None]
    mag = jnp.exp(lam_re * dt)
    ab_re = mag * jnp.cos(lam_im * dt)
    ab_im = mag * jnp.sin(lam_im * dt)
    den = lam_re * lam_re + lam_im * lam_im
    z_re = ((ab_re - 1.0) * lam_re + ab_im * lam_im) / den
    z_im = (ab_im * lam_re - (ab_re - 1.0) * lam_im) / den
    bb_re = z_re[..., None] * b_re - z_im[..., None] * b_im
    bb_im = z_re[..., None] * b_im + z_im[..., None] * b_re
    eye = jnp.eye(gpb, dtype=F32)

    def blockdiag_in(bb):
        bb = bb.reshape(nblk, gpb, P, H)
        m = jnp.einsum('jgph,gk->jghkp', bb, eye)
        return m.reshape(nblk, gpb * H, gpb * P)

    def blockdiag_out(cc):
        cc = cc.reshape(nblk, gpb, H, P)
        m = jnp.einsum('jghp,gk->jgpkh', cc, eye)
        return m.reshape(nblk, gpb * P, gpb * H)

    wb = jnp.concatenate([blockdiag_in(bb_re), blockdiag_in(bb_im)], axis=2).astype(BF16)
    wc = jnp.concatenate([blockdiag_out(c_re), blockdiag_out(-c_im)], axis=1).astype(BF16)

    def powers(n):
        n = n.astype(F32)[:, None, None]
        mg = jnp.exp(n * (lam_re * dt)[None])
        return ((mg * jnp.cos(n * (lam_im * dt)[None])).reshape(-1, G * P),
                (mg * jnp.sin(n * (lam_im * dt)[None])).reshape(-1, G * P))

    a1r, a1i = powers(jnp.ones((S5_SEGS,), jnp.int32))
    apr, api = powers(jnp.repeat(jnp.arange(tseg) + 1, S5_SEGS))
    n_join = int(math.log2(S5_SEGS))
    shifts = jnp.repeat(2 ** jnp.arange(n_join), S5_SEGS)
    asr, asi = powers(tseg * shifts)
    keep = (jnp.tile(jnp.arange(S5_SEGS), n_join) >= shifts)[:, None]
    return wb, wc, a1r, a1i, apr, api, jnp.where(keep, asr, 0.0), jnp.where(keep, asi, 0.0)


def _s5_state_to_cols(h_re, h_im, nblk):
    B, G, P = h_re.shape
    gpb = G // nblk
    st = jnp.stack([h_re.reshape(B, nblk, gpb * P), h_im.reshape(B, nblk, gpb * P)], axis=2)
    return st.reshape(B, 1, 2 * G * P)


def _s5_cols_to_state(cols, G, P, nblk):
    B = cols.shape[0]
    st = cols.reshape(B, nblk, 2, (G // nblk) * P)
    return st[:, :, 0].reshape(B, G, P), st[:, :, 1].reshape(B, G, P)


def _s5(u, h_re, h_im, lam_re, lam_im, log_dt, b_re, b_im, c_re, c_im, d_skip, w_glu_bf16):
    B, L, W = u.shape
    G, P = lam_re.shape
    T = _time_tile(L, 256)
    assert T % S5_SEGS == 0
    tseg = T // S5_SEGS
    nblk = W // LANES
    tables = _s5_tables(lam_re, lam_im, log_dt, b_re, b_im, c_re, c_im, tseg)
    r = jnp.arange(T)
    perm = ((r % S5_SEGS) * tseg + r // S5_SEGS)[:, None] == jnp.arange(T)[None, :]
    perm = perm.astype(BF16)
    h0 = _s5_state_to_cols(h_re, h_im, nblk)
    ncol = 2 * G * P
    consts = (_tile3(perm), perm.T) + tables + (d_skip.reshape(1, W), w_glu_bf16)
    y, hT = pl.pallas_call(
        functools.partial(_s5_kernel, tseg=tseg),
        out_shape=[jax.ShapeDtypeStruct((B, L, W), BF16), jax.ShapeDtypeStruct((B, 1, ncol), F32)],
        grid=(B, L // T),
        in_specs=[pl.BlockSpec((1, T, W), lambda b, t: (b, t, 0)),
                  pl.BlockSpec((1, 1, ncol), lambda b, t: (b, 0, 0))]
        + [_const_spec(c.shape) for c in consts],
        out_specs=[pl.BlockSpec((1, T, W), lambda b, t: (b, t, 0)),
                   pl.BlockSpec((1, 1, ncol), lambda b, t: (b, 0, 0))],
        scratch_shapes=[pltpu.VMEM((1, ncol), F32)],
        compiler_params=_cparams(("parallel", "arbitrary")),
        name="s5_mixer",
    )(u, h0, *consts)
    hr, hi = _s5_cols_to_state(hT[:, 0], G, P, nblk)
    return y, hr, hi


def _stack_heads(x, lane_lo):
    zero = jnp.zeros_like(x)
    return jnp.concatenate([jnp.where(lane_lo, x, zero), jnp.where(lane_lo, zero, x)], axis=0)


def _head_sum(x, hm2):
    nblk = x.shape[1] // LANES
    R = x.shape[0]
    xs = jnp.concatenate([x[:, j * LANES:(j + 1) * LANES] for j in range(nblk)], axis=0)
    s = _dot2(xs, hm2)
    return jnp.concatenate([s[j * R:(j + 1) * R] for j in range(nblk)], axis=1)


def _rwkv_kernel(p_ref, sp_ref, s0_ref, mu_ref, w0_ref, ww2_ref, a0_ref, wa2_ref, wg2_ref, kk_ref, ka_ref,
                 rk_ref, lng_ref, lnb_ref, hm_ref, tri_ref, y_ref, sh_ref, sT_ref, prev_ref, st_ref, *, width):
    t = pl.program_id(1)
    W = width
    NB, C, _ = p_ref.shape

    @pl.when(t == 0)
    def _():
        prev_ref[...] = sp_ref[...]
        st_ref[...] = s0_ref[...]

    rows = lax.broadcasted_iota(jnp.int32, (C, 1), 0)
    ps = []
    for n in range(NB):
        p = p_ref[n]
        p_prev = jnp.where(rows == 0, prev_ref[n], pltpu.roll(p, 1, axis=0))
        prev_ref[n] = p[C - 1:C, :]
        sh_ref[n] = p[C - 1:C, :]
        ps.append(p + (p_prev - p) * mu_ref[...])
    ps = jnp.concatenate(ps, axis=0)
    r = ps[:, :W]
    k = ps[:, W:2 * W]
    v = ps[:, 2 * W:3 * W]
    wa_low = ps[:, 3 * W:3 * W + LANES]
    g_low = ps[:, 3 * W + LANES:]
    xw = w0_ref[...] + _bdot(jnp.tanh(wa_low), ww2_ref[...])
    lw = -jnp.exp(-_softplus(-xw) - 0.5)
    a = _sigmoid(a0_ref[...] + _bdot(wa_low, wa2_ref[...]))
    g = _bdot(_sigmoid(g_low), wg2_ref[...])
    hm = hm_ref[...]
    kk = k * kk_ref[...]
    kk = kk / jnp.maximum(jnp.sqrt(_head_sum(kk * kk, hm)), 1e-12)
    kmod = k * (1.0 + (a - 1.0) * ka_ref[...])
    bvec = kk * a

    cum = _dot3(tri_ref[...], lw)
    e_in = jnp.exp(cum)
    e_ex = jnp.exp(cum - lw)
    e_neg = jnp.exp(-cum)
    cum_last = [cum[(n + 1) * C - 1:(n + 1) * C, :] for n in range(NB)]
    e_end = jnp.exp(jnp.concatenate([jnp.broadcast_to(c, (C, W)) for c in cum_last], axis=0) - cum)
    w_end = [jnp.exp(c) for c in cum_last]
    at = (-kk * e_ex).astype(BF16)
    rt = (r * e_in).astype(BF16)
    bt = (bvec * e_neg).astype(BF16)
    kt = (kmod * e_neg).astype(BF16)
    bw = (bvec * e_end).astype(BF16)
    kw = (kmod * e_end).astype(BF16)
    vb = v.astype(BF16)

    C2 = 2 * C
    si = lax.broadcasted_iota(jnp.int32, (C, C2), 0)
    sj = lax.broadcasted_iota(jnp.int32, (C, C2), 1) % C
    strict = si > sj
    lower = si >= sj
    lane_lo = lax.broadcasted_iota(jnp.int32, (1, LANES), 1) < RWKV_HEAD
    side_lo = lax.broadcasted_iota(jnp.int32, (1, C2), 1) < C
    npair = W // LANES
    chains = [(n, j) for n in range(NB) for j in range(npair)]
    x0, vst, bkw, l_ab, a_ak, a_r = {}, {}, {}, {}, {}, {}
    for c in chains:
        n, j = c
        pair = lambda x: x[n * C:(n + 1) * C, j * LANES:(j + 1) * LANES]
        ar = jnp.concatenate([pair(at), pair(rt)], axis=0)
        bk_st = jnp.concatenate([_stack_heads(pair(bt), lane_lo), _stack_heads(pair(kt), lane_lo)], axis=0)
        m = _bdot_nt(ar, bk_st)
        x0[c] = _bdot_nt(ar, st_ref[n, j])
        vst[c] = _stack_heads(pair(vb), lane_lo)
        bkw[c] = jnp.concatenate([_stack_heads(pair(bw), lane_lo), _stack_heads(pair(kw), lane_lo)], axis=0)
        l_ab[c] = jnp.where(strict, m[:C, :C2], 0.0)
        a_ak[c] = jnp.where(strict, m[:C, C2:], 0.0).astype(BF16)
        a_r[c] = jnp.concatenate([jnp.where(lower, m[C:, :C2], 0.0), jnp.where(lower, m[C:, C2:], 0.0)],
                                 axis=1).astype(BF16)
    def split(x):
        hi = x.astype(BF16)
        return hi, (x - hi.astype(F32)).astype(BF16)

    def lhs3(x):
        hi, lo = split(x)
        return jnp.concatenate([hi, lo, hi], axis=1)

    def rhs3(x, mask):
        hi, lo = split(x)
        hi = _stack_heads(hi, mask)
        return jnp.concatenate([hi, hi, _stack_heads(lo, mask)], axis=0)

    blk_t = si // RWKV_BASE
    blk_s = sj // RWKV_BASE
    diag8 = blk_t == blk_s
    tinv = {c: jnp.where(si == sj, 1.0, 0.0) + jnp.where(diag8, l_ab[c], 0.0) for c in chains}
    pw = {c: jnp.where(diag8, l_ab[c], 0.0) for c in chains}
    for _ in range(int(math.log2(RWKV_BASE)) - 1):
        for c in chains:
            pw[c] = _bdot(pw[c], _stack_heads(pw[c].astype(BF16), side_lo))
        for c in chains:
            tinv[c] = tinv[c] + _bdot(pw[c], _stack_heads(tinv[c].astype(BF16), side_lo))
    b = RWKV_BASE
    while b < C:
        lower_left = ((si // (2 * b)) == (sj // (2 * b))) & (((si // b) % 2) == 1) & (((sj // b) % 2) == 0)
        lt = {c: jnp.dot(lhs3(jnp.where(lower_left, l_ab[c], 0.0)), rhs3(tinv[c], side_lo),
                         preferred_element_type=F32) for c in chains}
        for c in chains:
            tinv[c] = tinv[c] + jnp.dot(lhs3(tinv[c]), rhs3(lt[c], side_lo), preferred_element_type=F32)
        b *= 2
    z = {c: x0[c][:C] + jnp.dot(a_ak[c], vst[c], preferred_element_type=F32) for c in chains}
    u = {c: jnp.dot(lhs3(tinv[c]), rhs3(z[c], lane_lo), preferred_element_type=F32) for c in chains}
    uv = {c: jnp.concatenate([_stack_heads(u[c].astype(BF16), lane_lo), vst[c]], axis=0) for c in chains}
    y_p = {c: x0[c][C:] + jnp.dot(a_r[c], uv[c], preferred_element_type=F32) for c in chains}
    for c in chains:
        n, j = c
        st_ref[n, j] = st_ref[n, j] * w_end[n][:, j * LANES:(j + 1) * LANES] + _bdot_tn(uv[c], bkw[c])
    y = jnp.concatenate([jnp.concatenate([y_p[(n, j)] for j in range(npair)], axis=1)
                         for n in range(NB)], axis=0)
    inv_n = 1.0 / RWKV_HEAD
    mean = _head_sum(y, hm) * inv_n
    yc = y - mean
    var = _head_sum(yc * yc, hm) * inv_n
    yn = yc * lax.rsqrt(var + RWKV_GN_EPS) * lng_ref[...] + lnb_ref[...]
    bonus = _head_sum(r * kmod * rk_ref[...], hm) * v
    out = ((yn + bonus) * g).astype(y_ref.dtype)
    for n in range(NB):
        y_ref[n] = out[n * C:(n + 1) * C]

    @pl.when(t == pl.num_programs(1) - 1)
    def _():
        sT_ref[...] = st_ref[...]


def _rwkv(p, shift_prev, s0, mu, w0, w_w2, a0, w_a2, w_g2, k_k, k_a, r_k, lnx_g, lnx_b):
    B, L, cols = p.shape
    H, N = s0.shape[1], s0.shape[2]
    W = H * N
    C = _time_tile(L, 64)
    assert 1 << int(math.log2(C)) == C and 2 * N == LANES
    npair = H // 2
    s0p = s0.reshape(B, npair, 2, N, N)
    z = jnp.zeros_like(s0p[:, :, 0])
    s0bd = jnp.concatenate([jnp.concatenate([s0p[:, :, 0], z], axis=-1),
                            jnp.concatenate([z, s0p[:, :, 1]], axis=-1)], axis=-2)
    lo = w_w2.shape[0]
    ww2p = jnp.concatenate([w_w2, jnp.zeros((LANES - lo, W), F32)], axis=0).astype(BF16)
    wa2p = jnp.concatenate([jnp.zeros((lo, W), F32), w_a2], axis=0).astype(BF16)
    hid = jnp.arange(LANES) // N
    hm = (hid[:, None] == hid[None, :]).astype(BF16)
    hm2 = jnp.concatenate([hm, hm], axis=0)
    nb = RWKV_SEQS if B % RWKV_SEQS == 0 else 1
    tr = jnp.arange(nb * C)
    tri = ((tr[:, None] >= tr[None, :]) & (tr[:, None] // C == tr[None, :] // C)).astype(BF16)
    row = lambda x: x.reshape(1, -1)
    y, sh, sT = pl.pallas_call(
        functools.partial(_rwkv_kernel, width=W),
        out_shape=[jax.ShapeDtypeStruct((B, L, W), BF16),
                   jax.ShapeDtypeStruct((B, 1, cols), F32),
                   jax.ShapeDtypeStruct((B, npair, LANES, LANES), F32)],
        grid=(B // nb, L // C),
        in_specs=[pl.BlockSpec((nb, C, cols), lambda b, t: (b, t, 0)),
                  pl.BlockSpec((nb, 1, cols), lambda b, t: (b, 0, 0)),
                  pl.BlockSpec((nb, npair, LANES, LANES), lambda b, t: (b, 0, 0, 0)),
                  _const_spec((1, cols)), _const_spec((1, W)), _const_spec((LANES, W)), _const_spec((1, W)),
                  _const_spec((LANES, W)), _const_spec((LANES, W)), _const_spec((1, W)), _const_spec((1, W)),
                  _const_spec((1, W)), _const_spec((1, W)), _const_spec((1, W)), _const_spec((2 * LANES, LANES)),
                  _const_spec((nb * C, 3 * nb * C))],
        out_specs=[pl.BlockSpec((nb, C, W), lambda b, t: (b, t, 0)),
                   pl.BlockSpec((nb, 1, cols), lambda b, t: (b, 0, 0)),
                   pl.BlockSpec((nb, npair, LANES, LANES), lambda b, t: (b, 0, 0, 0))],
        scratch_shapes=[pltpu.VMEM((nb, 1, cols), F32), pltpu.VMEM((nb, npair, LANES, LANES), F32)],
        compiler_params=_cparams(("parallel", "arbitrary")),
        name="rwkv7_mixer",
    )(p, shift_prev.reshape(B, 1, cols), s0bd, row(mu), row(w0), ww2p, row(a0), wa2p, w_g2.astype(BF16),
      row(k_k), row(k_a), row(r_k), row(lnx_g), row(lnx_b), hm2, _tile3(tri))
    sT = sT.reshape(B, npair, 2, N, 2, N)
    s_new = jnp.stack([sT[:, :, 0, :, 0, :], sT[:, :, 1, :, 1, :]], axis=2).reshape(B, H, N, N)
    return y, sh[:, 0], s_new


def _hgrn_segment_matrix(C):
    t = jnp.arange(C)[:, None]
    r = jnp.arange(C)[None, :]
    blocks = [r <= t]
    h = C // 2
    while h >= 1:
        mid = (t // (2 * h)) * (2 * h) + h - 1
        is_q = (t & h) != 0
        blocks.append(jnp.where(is_q, (r > mid) & (r <= t), (r > t) & (r <= mid)))
        h //= 2
    return jnp.concatenate(blocks, axis=0).astype(BF16)


def _hgrn_kernel(p_ref, s0_ref, llb_ref, l1lb_ref, omlb_ref, gn_ref, mseg_ref, y_ref, sT_ref, st_ref, *, width):
    t = pl.program_id(1)
    W = width
    H = W // HGRN_HEAD
    NB, C, _ = p_ref.shape

    @pl.when(t == 0)
    def _():
        for n in range(NB):
            for h in range(H):
                st_ref[n, h] = s0_ref[n, h].T

    p = jnp.concatenate([p_ref[n] for n in range(NB)], axis=0)
    q = _silu(p[:, :W])
    f = p[:, W:2 * W]
    v = p[:, 2 * W:3 * W]
    g = p[:, 3 * W:]
    ef = jnp.exp(-jnp.abs(f))
    lsig = jnp.minimum(f, 0.0) - jnp.log(1.0 + ef)
    c1 = llb_ref[...]
    c2 = l1lb_ref[...] + lsig
    log_f = jnp.maximum(c1, c2) + jnp.log(1.0 + jnp.exp(-jnp.abs(c1 - c2)))
    kv = omlb_ref[...] * (jnp.where(f >= 0.0, ef, 1.0) / (1.0 + ef))
    d_all = [_dot3(mseg_ref[...], log_f[n * C:(n + 1) * C]) for n in range(NB)]
    b = jnp.concatenate([d[:C] for d in d_all], axis=0)
    b_last = [d[C - 1:C, :] for d in d_all]
    qe = (q * jnp.exp(b)).astype(BF16)
    ke = (kv * jnp.exp(jnp.concatenate([jnp.broadcast_to(bl, (C, W)) for bl in b_last], axis=0) - b)).astype(BF16)
    e_last = [jnp.exp(bl) for bl in b_last]
    qk = q * kv
    vb = v.astype(BF16)
    rowi = lax.broadcasted_iota(jnp.int32, (NB * C, 1), 0)
    ti = lax.broadcasted_iota(jnp.int32, (C, C), 0)
    tj = lax.broadcasted_iota(jnp.int32, (C, C), 1)
    zs, masks = [], []
    hsz = C // 2
    lvl = 1
    while hsz >= 1:
        is_q = (rowi & hsz) != 0
        seg = jnp.concatenate([d[lvl * C:(lvl + 1) * C] for d in d_all], axis=0)
        zs.append((jnp.exp(seg) * jnp.where(is_q, q, kv)).astype(BF16))
        masks.append(((ti & hsz) != 0) & ((tj & hsz) == 0) & ((ti // (2 * hsz)) == (tj // (2 * hsz))))
        hsz //= 2
        lvl += 1
    chains = [(n, h) for n in range(NB) for h in range(H)]
    blk = lambda x, c: x[c[0] * C:(c[0] + 1) * C, c[1] * HGRN_HEAD:(c[1] + 1) * HGRN_HEAD]
    att = {}
    for z, mask in zip(zs, masks):
        for c in chains:
            zh = blk(z, c)
            term = jnp.where(mask, _bdot_nt(zh, zh), 0.0)
            att[c] = term if c not in att else att[c] + term
    o = {c: (_bdot(att[c], blk(vb, c)) + _bdot_nt(blk(qe, c), st_ref[c[0], c[1]])
             + jnp.sum(blk(qk, c), axis=-1, keepdims=True) * blk(v, c)) for c in chains}
    for c in chains:
        n, h = c
        st_ref[n, h] = (st_ref[n, h] * e_last[n][:, h * HGRN_HEAD:(h + 1) * HGRN_HEAD]
                        + _bdot_tn(blk(vb, c), blk(ke, c)))
    for c in chains:
        o[c] = o[c] * lax.rsqrt(jnp.mean(o[c] * o[c], axis=-1, keepdims=True) + NORM_EPS) * gn_ref[...]
    o = jnp.concatenate([jnp.concatenate([o[(n, h)] for h in range(H)], axis=1) for n in range(NB)], axis=0)
    out = (o * _silu(g)).astype(y_ref.dtype)
    for n in range(NB):
        y_ref[n] = out[n * C:(n + 1) * C]

    @pl.when(t == pl.num_programs(1) - 1)
    def _():
        for n in range(NB):
            for h in range(H):
                sT_ref[n, h] = st_ref[n, h].T


def _hgrn(p, s0, lb, gnorm):
    B, L, cols = p.shape
    W = cols // 4
    H = W // HGRN_HEAD
    C = _time_tile(L, 64)
    assert 1 << int(math.log2(C)) == C
    mseg = _tile3(_hgrn_segment_matrix(C))
    nb = HGRN_SEQS if B % HGRN_SEQS == 0 else 1
    row = lambda x: x.reshape(1, -1)
    y, sT = pl.pallas_call(
        functools.partial(_hgrn_kernel, width=W),
        out_shape=[jax.ShapeDtypeStruct((B, L, W), BF16),
                   jax.ShapeDtypeStruct((B, H, HGRN_HEAD, HGRN_HEAD), F32)],
        grid=(B // nb, L // C),
        in_specs=[pl.BlockSpec((nb, C, cols), lambda b, t: (b, t, 0)),
                  pl.BlockSpec((nb, H, HGRN_HEAD, HGRN_HEAD), lambda b, t: (b, 0, 0, 0)),
                  _const_spec((1, W)), _const_spec((1, W)), _const_spec((1, W)), _const_spec((1, HGRN_HEAD)),
                  _const_spec(mseg.shape)],
        out_specs=[pl.BlockSpec((nb, C, W), lambda b, t: (b, t, 0)),
                   pl.BlockSpec((nb, H, HGRN_HEAD, HGRN_HEAD), lambda b, t: (b, 0, 0, 0))],
        scratch_shapes=[pltpu.VMEM((nb, H, HGRN_HEAD, HGRN_HEAD), F32)],
        compiler_params=_cparams(("parallel", "arbitrary")),
        name="hgrn2_mixer",
    )(p, s0, row(jnp.log(lb)), row(jnp.log1p(-lb)), row(1.0 - lb), row(gnorm), mseg)
    return y, sT


def _trunk(x, mods_all, s5_re, s5_im, rwkv_s, rwkv_shift, hgrn_s, P, Pb):
    depth = P['norm_mix'].shape[0]
    lb_all = jax.nn.softmax(P['hgrn_lower_bounds'], axis=0)
    lb_all = jnp.cumsum(lb_all, axis=0) - lb_all[0]
    s5_width = P['s5_w_glu'].shape[1]
    n_re, n_im, n_rw, n_sh, n_hg = [], [], [], [], []
    B, L, D = x.shape
    flat = L < FLAT_BELOW
    if flat:
        x = x.reshape(1, B * L, D)
        mods_all = jnp.repeat(mods_all, L, axis=1).transpose(0, 2, 1, 3)[:, None]
    else:
        mods_all = mods_all[:, :, :, None, :]
    seq = lambda a: a.reshape(B, L, a.shape[-1])
    tok = lambda a: a.reshape(x.shape[0], x.shape[1], a.shape[-1])
    for l in range(depth):
        mods = mods_all[l]
        x = _ffn(x, mods, P['norm_ffn1'][l], Pb['ffn1_w1'][l], Pb['ffn1_w3'][l], Pb['ffn1_w2'][l], 0)
        i = l // 2
        if l % 2 == 0:
            n_in = Pb['ab_w_in'].shape[2]
            u, pr = _inproj(x, mods, P['norm_mix'][l], Pb['ab_w_in'][i], (s5_width, n_in - s5_width))
            ya, hr, hi = _s5(seq(u), s5_re[i], s5_im[i], P['s5_lam_re'][i], P['s5_lam_im'][i],
                             P['s5_log_dt'][i], P['s5_b_re'][i], P['s5_b_im'][i], P['s5_c_re'][i],
                             P['s5_c_im'][i], P['s5_d'][i], Pb['s5_w_glu'][i])
            yb, sh_new, rw_new = _rwkv(seq(pr), rwkv_shift[i], rwkv_s[i], P['rwkv_mu'][i], P['rwkv_w0'][i],
                                       P['rwkv_w_w2'][i], P['rwkv_a0'][i], P['rwkv_w_a2'][i],
                                       P['rwkv_w_g2'][i], P['rwkv_k_k'][i], P['rwkv_k_a'][i],
                                       P['rwkv_r_k'][i], P['rwkv_lnx_g'][i], P['rwkv_lnx_b'][i])
            ys, w_out = (tok(ya), tok(yb)), Pb['ab_w_out'][i]
            n_re.append(hr)
            n_im.append(hi)
            n_rw.append(rw_new)
            n_sh.append(sh_new)
        else:
            (pc,) = _inproj(x, mods, P['norm_mix'][l], Pb['c_w_in'][i], (Pb['c_w_in'].shape[2],))
            yc, hg_new = _hgrn(seq(pc), hgrn_s[i], lb_all[l], P['hgrn_gnorm'][i])
            ys, w_out = (tok(yc),), Pb['c_w_out'][i]
            n_hg.append(hg_new)
        fin = P['final_norm'] if l == depth - 1 else None
        x = _ffn(x, mods, P['norm_ffn2'][l], Pb['ffn2_w1'][l], Pb['ffn2_w3'][l], Pb['ffn2_w2'][l], 2,
                 final_gain=fin, ys=ys, w_out=w_out)
    return (x.reshape(B, L, D), jnp.stack(n_re), jnp.stack(n_im), jnp.stack(n_rw), jnp.stack(n_sh),
            jnp.stack(n_hg))


def kernel(x_prompt, x_sample, state_s5_re, state_s5_im, state_rwkv, state_rwkv_shift, state_hgrn, c_prompt, c_sample, w_ada, b_ada, norm_ffn1, norm_mix, norm_ffn2, ffn1_w1, ffn1_w3, ffn1_w2, ffn2_w1, ffn2_w3, ffn2_w2, ab_w_in, ab_w_out, s5_lam_re, s5_lam_im, s5_log_dt, s5_b_re, s5_b_im, s5_c_re, s5_c_im, s5_d, s5_w_glu, rwkv_mu, rwkv_w0, rwkv_w_w2, rwkv_a0, rwkv_w_a2, rwkv_w_g2, rwkv_k_k, rwkv_k_a, rwkv_r_k, rwkv_lnx_g, rwkv_lnx_b, c_w_in, c_w_out, hgrn_lower_bounds, hgrn_gnorm, final_norm):
    P = dict(norm_ffn1=norm_ffn1, norm_mix=norm_mix, norm_ffn2=norm_ffn2,
             s5_lam_re=s5_lam_re, s5_lam_im=s5_lam_im, s5_log_dt=s5_log_dt, s5_b_re=s5_b_re, s5_b_im=s5_b_im,
             s5_c_re=s5_c_re, s5_c_im=s5_c_im, s5_d=s5_d, s5_w_glu=s5_w_glu, rwkv_mu=rwkv_mu,
             rwkv_w0=rwkv_w0, rwkv_w_w2=rwkv_w_w2, rwkv_a0=rwkv_a0, rwkv_w_a2=rwkv_w_a2, rwkv_w_g2=rwkv_w_g2,
             rwkv_k_k=rwkv_k_k, rwkv_k_a=rwkv_k_a, rwkv_r_k=rwkv_r_k, rwkv_lnx_g=rwkv_lnx_g,
             rwkv_lnx_b=rwkv_lnx_b, hgrn_lower_bounds=hgrn_lower_bounds, hgrn_gnorm=hgrn_gnorm,
             final_norm=final_norm)
    Pb = {n: w.astype(BF16) for n, w in dict(
        ffn1_w1=ffn1_w1, ffn1_w3=ffn1_w3, ffn1_w2=ffn1_w2, ffn2_w1=ffn2_w1, ffn2_w3=ffn2_w3, ffn2_w2=ffn2_w2,
        ab_w_in=ab_w_in, ab_w_out=ab_w_out, c_w_in=c_w_in, c_w_out=c_w_out, s5_w_glu=s5_w_glu).items()}
    depth, d = norm_mix.shape
    bp, bs = x_prompt.shape[0], x_sample.shape[0]
    mods = _ada(jnp.concatenate([c_prompt, c_sample], axis=0), w_ada, b_ada)
    mods = mods.reshape(depth, bp + bs, N_MOD, d)
    dt = x_prompt.dtype
    zeros_like_b = lambda s: jnp.zeros((s.shape[0], bp) + s.shape[2:], dt)
    outs_p = _trunk(x_prompt, mods[:, :bp], zeros_like_b(state_s5_re), zeros_like_b(state_s5_im),
                    zeros_like_b(state_rwkv), zeros_like_b(state_rwkv_shift), zeros_like_b(state_hgrn), P, Pb)
    outs_s = _trunk(x_sample, mods[:, bp:], state_s5_re, state_s5_im, state_rwkv, state_rwkv_shift,
                    state_hgrn, P, Pb)
    return (outs_p[0], outs_s[0]) + tuple(outs_p[1:]) + tuple(outs_s[1:])
```

```python
import functools
import math

import jax
import jax.numpy as jnp
from jax import lax
from jax.experimental import pallas as pl
from jax.experimental.pallas import tpu as pltpu

F32 = jnp.float32
BF16 = jnp.bfloat16

NORM_EPS = 1e-6
RWKV_GN_EPS = 64e-5
N_MOD = 9
RWKV_HEAD = 64
RWKV_BASE = 8
HGRN_HEAD = 128
LANES = 128
S5_SEQS = 8
S5_STEPS = 64
RWKV_SEQS = 4
HGRN_SEQS = 4
VMEM_LIMIT = 56 * 1024 * 1024
FFN_ROWS = 1024
FLAT_BELOW = 64


def _bdot(a, b):
    return jnp.dot(a.astype(BF16), b.astype(BF16), preferred_element_type=F32)


def _bdot_nt(a, b):
    return lax.dot_general(a.astype(BF16), b.astype(BF16), (((1,), (1,)), ((), ())),
                           preferred_element_type=F32)


def _bdot_tn(a, b):
    return lax.dot_general(a.astype(BF16), b.astype(BF16), (((0,), (0,)), ((), ())),
                           preferred_element_type=F32)


def _dot2(a, b2):
    hi = a.astype(BF16)
    lo = (a - hi.astype(F32)).astype(BF16)
    return jnp.dot(jnp.concatenate([hi, lo], axis=1), b2, preferred_element_type=F32)


def _dot3(m3, x):
    hi = x.astype(BF16)
    r1 = x - hi.astype(F32)
    mid = r1.astype(BF16)
    lo = (r1 - mid.astype(F32)).astype(BF16)
    return jnp.dot(m3, jnp.concatenate([hi, mid, lo], axis=0), preferred_element_type=F32)


def _tile3(m):
    return jnp.concatenate([m, m, m], axis=1)


def _sigmoid(x):
    return 1.0 / (1.0 + jnp.exp(-x))


def _silu(x):
    return x * _sigmoid(x)


def _softplus(x):
    return jnp.maximum(x, 0.0) + jnp.log(1.0 + jnp.exp(-jnp.abs(x)))


def _cparams(sem):
    return pltpu.CompilerParams(dimension_semantics=sem, vmem_limit_bytes=VMEM_LIMIT)


def _const_spec(shape):
    nd = len(shape)
    return pl.BlockSpec(shape, lambda *_: (0,) * nd, pipeline_mode=pl.Buffered(1))


def _time_tile(L, want):
    t = min(L, want)
    assert L % t == 0 and t % 8 == 0
    return t


def _ada_kernel(c_ref, w_ref, b_ref, o_ref):
    cs = _silu(c_ref[...])
    o_ref[0] = _bdot(cs, w_ref[0]) + b_ref[0]


def _ada(c_all, w_ada, b_ada):
    depth, d, n = w_ada.shape
    nb = c_all.shape[0]
    tn = 1024
    return pl.pallas_call(
        _ada_kernel,
        out_shape=jax.ShapeDtypeStruct((depth, nb, n), F32),
        grid=(depth, n // tn),
        in_specs=[pl.BlockSpec((nb, d), lambda l, j: (0, 0)),
                  pl.BlockSpec((1, d, tn), lambda l, j: (l, 0, j)),
                  pl.BlockSpec((1, 1, tn), lambda l, j: (l, 0, j))],
        out_specs=pl.BlockSpec((1, nb, tn), lambda l, j: (l, 0, j)),
        compiler_params=_cparams(("arbitrary", "arbitrary")),
        name="ada_mod",
    )(c_all, w_ada, b_ada.reshape(depth, 1, n))


def _norm_mod(x, gain, shift, scale):
    y = x * lax.rsqrt(jnp.mean(x * x, axis=-1, keepdims=True) + NORM_EPS) * gain
    return y * (1.0 + scale) + shift


def _ffn_kernel(x_ref, mod_ref, g_ref, w1_ref, w3_ref, w2_ref, *rest, which, fc, final, n_y):
    y_refs = rest[:n_y]
    rest = rest[n_y:]
    if n_y:
        wo_ref, rest = rest[0], rest[1:]
    if final:
        gf_ref, o_ref = rest
    else:
        (o_ref,) = rest
    x = x_ref[0]
    if n_y:
        off = 0
        mix = None
        for y_ref in y_refs:
            k = y_ref.shape[2]
            part = jnp.dot(y_ref[0], wo_ref[off:off + k, :], preferred_element_type=F32)
            mix = part if mix is None else mix + part
            off += k
        x = x + (1.0 + mod_ref[0, 5]) * mix
    shift = mod_ref[0, 3 * which]
    scale = mod_ref[0, 3 * which + 1]
    gate = mod_ref[0, 3 * which + 2]
    h = _norm_mod(x, g_ref[...], shift, scale).astype(BF16)
    d_ff = w1_ref.shape[1]
    acc = jnp.zeros(x.shape, F32)
    for c in range(d_ff // fc):
        a = jnp.dot(h, w1_ref[:, c * fc:(c + 1) * fc], preferred_element_type=F32)
        b = jnp.dot(h, w3_ref[:, c * fc:(c + 1) * fc], preferred_element_type=F32)
        u = (_silu(a) * b).astype(BF16)
        acc = acc + jnp.dot(u, w2_ref[c * fc:(c + 1) * fc, :], preferred_element_type=F32)
    y = x + 0.5 * (1.0 + gate) * acc
    if final:
        y = y * lax.rsqrt(jnp.mean(y * y, axis=-1, keepdims=True) + NORM_EPS) * gf_ref[...]
    o_ref[0] = y


def _mod_spec(mods, tm):
    _, n_mod, R, D = mods.shape
    if R == 1:
        return pl.BlockSpec((1, n_mod, 1, D), lambda b, i: (b, 0, 0, 0))
    return pl.BlockSpec((1, n_mod, tm, D), lambda b, i: (b, 0, i, 0))


def _ffn(x, mods, gain, w1, w3, w2, which, final_gain=None, ys=(), w_out=None):
    B, L, D = x.shape
    d_ff = w1.shape[1]
    tm = _time_tile(L, FFN_ROWS)
    fc = 256
    assert d_ff % fc == 0
    final = final_gain is not None
    ins = [x, mods, gain.reshape(1, D), w1, w3, w2]
    specs = [pl.BlockSpec((1, tm, D), lambda b, i: (b, i, 0)), _mod_spec(mods, tm),
             _const_spec((1, D)), _const_spec((D, d_ff)), _const_spec((D, d_ff)), _const_spec((d_ff, D))]
    if ys:
        assert sum(y.shape[2] for y in ys) == w_out.shape[0]
        ins += list(ys) + [w_out]
        specs += [pl.BlockSpec((1, tm, y.shape[2]), lambda b, i: (b, i, 0)) for y in ys]
        specs.append(_const_spec(w_out.shape))
    if final:
        ins.append(final_gain.reshape(1, D))
        specs.append(_const_spec((1, D)))
    return pl.pallas_call(
        functools.partial(_ffn_kernel, which=which, fc=fc, final=final, n_y=len(ys)),
        out_shape=jax.ShapeDtypeStruct((B, L, D), F32),
        grid=(B, L // tm),
        in_specs=specs,
        out_specs=pl.BlockSpec((1, tm, D), lambda b, i: (b, i, 0)),
        compiler_params=_cparams(("parallel", "parallel")),
        name="ffn",
    )(*ins)


def _inproj_kernel(x_ref, mod_ref, g_ref, w_ref, *o_refs, splits):
    x = x_ref[0]
    h = _norm_mod(x, g_ref[...], mod_ref[0, 3], mod_ref[0, 4]).astype(BF16)
    off = 0
    for o_ref, n in zip(o_refs, splits):
        o_ref[0] = jnp.dot(h, w_ref[:, off:off + n], preferred_element_type=F32)
        off += n


def _inproj(x, mods, gain, w_in, splits):
    B, L, D = x.shape
    n = w_in.shape[1]
    assert sum(splits) == n
    tm = _time_tile(L, FFN_ROWS)
    return pl.pallas_call(
        functools.partial(_inproj_kernel, splits=splits),
        out_shape=[jax.ShapeDtypeStruct((B, L, s), F32) for s in splits],
        grid=(B, L // tm),
        in_specs=[pl.BlockSpec((1, tm, D), lambda b, i: (b, i, 0)), _mod_spec(mods, tm),
                  _const_spec((1, D)), _const_spec((D, n))],
        out_specs=[pl.BlockSpec((1, tm, s), lambda b, i: (b, i, 0)) for s in splits],
        compiler_params=_cparams(("parallel", "parallel")),
        name="mixer_in_proj",
    )(x, mods, gain.reshape(1, D), w_in)


def _s5_kernel(u_ref, h0_ref, perm_ref, permt_ref, wb_ref, wc_ref, a1r_ref, a1i_ref, d_ref, wg_ref,
               y_ref, hT_ref, carry_ref, *, tseg):
    t = pl.program_id(1)

    @pl.when(t == 0)
    def _():
        carry_ref[...] = h0_ref[...]

    u = jnp.concatenate([u_ref[k] for k in range(S5_SEQS)], axis=0)
    u = _dot3(perm_ref[...], u)
    nblk = u.shape[1] // LANES
    half = wb_ref.shape[2] // 2
    ys = []
    for j in range(nblk):
        cols = slice(j * half, (j + 1) * half)
        re_cols = slice(2 * j * half, (2 * j + 1) * half)
        im_cols = slice((2 * j + 1) * half, (2 * j + 2) * half)
        bu = _bdot(u[:, j * LANES:(j + 1) * LANES], wb_ref[j])
        ar = a1r_ref[:, cols]
        ai = a1i_ref[:, cols]
        sr = carry_ref[:, re_cols]
        si = carry_ref[:, im_cols]
        xr_rows, xi_rows = [], []
        for i in range(tseg):
            br = bu[S5_SEQS * i:S5_SEQS * (i + 1), :half]
            bi = bu[S5_SEQS * i:S5_SEQS * (i + 1), half:]
            sr, si = ar * sr - ai * si + br, ar * si + ai * sr + bi
            xr_rows.append(sr)
            xi_rows.append(si)
        carry_ref[:, re_cols] = sr
        carry_ref[:, im_cols] = si
        xr = jnp.concatenate(xr_rows, axis=0)
        xi = jnp.concatenate(xi_rows, axis=0)
        ys.append(_bdot(xr, wc_ref[j, :half, :]) + _bdot(xi, wc_ref[j, half:, :]))
    y = jnp.concatenate(ys, axis=1) + u * d_ref[...]
    y = jax.nn.gelu(y)
    y = (y * _sigmoid(_bdot(y, wg_ref[...]))).astype(BF16)
    y = jnp.dot(permt_ref[...], y, preferred_element_type=F32).astype(y_ref.dtype)
    for k in range(S5_SEQS):
        y_ref[k] = y[k * tseg:(k + 1) * tseg]
    hT_ref[...] = carry_ref[...]


def _s5_tables(lam_re, lam_im, log_dt, b_re, b_im, c_re, c_im):
    G, P = lam_re.shape
    H = b_re.shape[2]
    gpb = LANES // H
    nblk = G // gpb
    dt = jnp.exp(log_dt)[:, None]
    mag = jnp.exp(lam_re * dt)
    ab_re = mag * jnp.cos(lam_im * dt)
    ab_im = mag * jnp.sin(lam_im * dt)
    den = lam_re * lam_re + lam_im * lam_im
    z_re = ((ab_re - 1.0) * lam_re + ab_im * lam_im) / den
    z_im = (ab_im * lam_re - (ab_re - 1.0) * lam_im) / den
    bb_re = z_re[..., None] * b_re - z_im[..., None] * b_im
    bb_im = z_re[..., None] * b_im + z_im[..., None] * b_re
    eye = jnp.eye(gpb, dtype=F32)

    def blockdiag_in(bb):
        bb = bb.reshape(nblk, gpb, P, H)
        m = jnp.einsum('jgph,gk->jghkp', bb, eye)
        return m.reshape(nblk, gpb * H, gpb * P)

    def blockdiag_out(cc):
        cc = cc.reshape(nblk, gpb, H, P)
        m = jnp.einsum('jghp,gk->jgpkh', cc, eye)
        return m.reshape(nblk, gpb * P, gpb * H)

    wb = jnp.concatenate([blockdiag_in(bb_re), blockdiag_in(bb_im)], axis=2).astype(BF16)
    wc = jnp.concatenate([blockdiag_out(c_re), blockdiag_out(-c_im)], axis=1).astype(BF16)

    a1r = jnp.broadcast_to(ab_re.reshape(1, G * P), (S5_SEQS, G * P))
    a1i = jnp.broadcast_to(ab_im.reshape(1, G * P), (S5_SEQS, G * P))
    return wb, wc, a1r, a1i


def _s5_state_to_cols(h_re, h_im, nblk):
    B, G, P = h_re.shape
    gpb = G // nblk
    st = jnp.stack([h_re.reshape(B, nblk, gpb * P), h_im.reshape(B, nblk, gpb * P)], axis=2)
    return st.reshape(B, 2 * G * P)


def _s5_cols_to_state(cols, G, P, nblk):
    B = cols.shape[0]
    st = cols.reshape(B, nblk, 2, (G // nblk) * P)
    return st[:, :, 0].reshape(B, G, P), st[:, :, 1].reshape(B, G, P)


def _s5(u, h_re, h_im, lam_re, lam_im, log_dt, b_re, b_im, c_re, c_im, d_skip, w_glu_bf16):
    B, L, W = u.shape
    G, P = lam_re.shape
    assert B % S5_SEQS == 0
    tseg = _time_tile(L, S5_STEPS)
    T = S5_SEQS * tseg
    nblk = W // LANES
    tables = _s5_tables(lam_re, lam_im, log_dt, b_re, b_im, c_re, c_im)
    r = jnp.arange(T)
    perm = ((r % S5_SEQS) * tseg + r // S5_SEQS)[:, None] == jnp.arange(T)[None, :]
    perm = perm.astype(BF16)
    h0 = _s5_state_to_cols(h_re, h_im, nblk)
    ncol = 2 * G * P
    consts = (_tile3(perm), perm.T) + tables + (d_skip.reshape(1, W), w_glu_bf16)
    y, hT = pl.pallas_call(
        functools.partial(_s5_kernel, tseg=tseg),
        out_shape=[jax.ShapeDtypeStruct((B, L, W), BF16), jax.ShapeDtypeStruct((B, ncol), F32)],
        grid=(B // S5_SEQS, L // tseg),
        in_specs=[pl.BlockSpec((S5_SEQS, tseg, W), lambda b, t: (b, t, 0)),
                  pl.BlockSpec((S5_SEQS, ncol), lambda b, t: (b, 0))]
        + [_const_spec(c.shape) for c in consts],
        out_specs=[pl.BlockSpec((S5_SEQS, tseg, W), lambda b, t: (b, t, 0)),
                   pl.BlockSpec((S5_SEQS, ncol), lambda b, t: (b, 0))],
        scratch_shapes=[pltpu.VMEM((S5_SEQS, ncol), F32)],
        compiler_params=_cparams(("parallel", "arbitrary")),
        name="s5_mixer",
    )(u, h0, *consts)
    hr, hi = _s5_cols_to_state(hT, G, P, nblk)
    return y, hr, hi


def _stack_heads(x, lane_lo):
    zero = jnp.zeros_like(x)
    return jnp.concatenate([jnp.where(lane_lo, x, zero), jnp.where(lane_lo, zero, x)], axis=0)


def _head_sum(x, hm2):
    nblk = x.shape[1] // LANES
    R = x.shape[0]
    xs = jnp.concatenate([x[:, j * LANES:(j + 1) * LANES] for j in range(nblk)], axis=0)
    s = _dot2(xs, hm2)
    return jnp.concatenate([s[j * R:(j + 1) * R] for j in range(nblk)], axis=1)


def _rwkv_kernel(p_ref, sp_ref, s0_ref, mu_ref, w0_ref, ww2_ref, a0_ref, wa2_ref, wg2_ref, kk_ref, ka_ref,
                 rk_ref, lng_ref, lnb_ref, hm_ref, tri_ref, y_ref, sh_ref, sT_ref, prev_ref, st_ref, *, width):
    t = pl.program_id(1)
    W = width
    NB, C, _ = p_ref.shape

    @pl.when(t == 0)
    def _():
        prev_ref[...] = sp_ref[...]
        st_ref[...] = s0_ref[...]

    rows = lax.broadcasted_iota(jnp.int32, (C, 1), 0)
    ps = []
    for n in range(NB):
        p = p_ref[n]
        p_prev = jnp.where(rows == 0, prev_ref[n], pltpu.roll(p, 1, axis=0))
        prev_ref[n] = p[C - 1:C, :]
        sh_ref[n] = p[C - 1:C, :]
        ps.append(p + (p_prev - p) * mu_ref[...])
    ps = jnp.concatenate(ps, axis=0)
    r = ps[:, :W]
    k = ps[:, W:2 * W]
    v = ps[:, 2 * W:3 * W]
    wa_low = ps[:, 3 * W:3 * W + LANES]
    g_low = ps[:, 3 * W + LANES:]
    xw = w0_ref[...] + _bdot(jnp.tanh(wa_low), ww2_ref[...])
    lw = -jnp.exp(-_softplus(-xw) - 0.5)
    a = _sigmoid(a0_ref[...] + _bdot(wa_low, wa2_ref[...]))
    g = _bdot(_sigmoid(g_low), wg2_ref[...])
    hm = hm_ref[...]
    kk = k * kk_ref[...]
    kk = kk / jnp.maximum(jnp.sqrt(_head_sum(kk * kk, hm)), 1e-12)
    kmod = k * (1.0 + (a - 1.0) * ka_ref[...])
    bvec = kk * a

    cum = _dot3(tri_ref[...], lw)
    e_in = jnp.exp(cum)
    e_ex = jnp.exp(cum - lw)
    e_neg = jnp.exp(-cum)
    cum_last = [cum[(n + 1) * C - 1:(n + 1) * C, :] for n in range(NB)]
    e_end = jnp.exp(jnp.concatenate([jnp.broadcast_to(c, (C, W)) for c in cum_last], axis=0) - cum)
    w_end = [jnp.exp(c) for c in cum_last]
    at = (-kk * e_ex).astype(BF16)
    rt = (r * e_in).astype(BF16)
    bt = (bvec * e_neg).astype(BF16)
    kt = (kmod * e_neg).astype(BF16)
    bw = (bvec * e_end).astype(BF16)
    kw = (kmod * e_end).astype(BF16)
    vb = v.astype(BF16)

    C2 = 2 * C
    si = lax.broadcasted_iota(jnp.int32, (C, C2), 0)
    sj = lax.broadcasted_iota(jnp.int32, (C, C2), 1) % C
    strict = si > sj
    lower = si >= sj
    lane_lo = lax.broadcasted_iota(jnp.int32, (1, LANES), 1) < RWKV_HEAD
    side_lo = lax.broadcasted_iota(jnp.int32, (1, C2), 1) < C
    npair = W // LANES
    chains = [(n, j) for n in range(NB) for j in range(npair)]
    x0, vst, bkw, l_ab, a_ak, a_r = {}, {}, {}, {}, {}, {}
    for c in chains:
        n, j = c
        pair = lambda x: x[n * C:(n + 1) * C, j * LANES:(j + 1) * LANES]
        ar = jnp.concatenate([pair(at), pair(rt)], axis=0)
        bk_st = jnp.concatenate([_stack_heads(pair(bt), lane_lo), _stack_heads(pair(kt), lane_lo)], axis=0)
        m = _bdot_nt(ar, bk_st)
        x0[c] = _bdot_nt(ar, st_ref[n, j])
        vst[c] = _stack_heads(pair(vb), lane_lo)
        bkw[c] = jnp.concatenate([_stack_heads(pair(bw), lane_lo), _stack_heads(pair(kw), lane_lo)], axis=0)
        l_ab[c] = jnp.where(strict, m[:C, :C2], 0.0)
        a_ak[c] = jnp.where(strict, m[:C, C2:], 0.0).astype(BF16)
        a_r[c] = jnp.concatenate([jnp.where(lower, m[C:, :C2], 0.0), jnp.where(lower, m[C:, C2:], 0.0)],
                                 axis=1).astype(BF16)
    def split(x):
        hi = x.astype(BF16)
        return hi, (x - hi.astype(F32)).astype(BF16)

    def lhs3(x):
        hi, lo = split(x)
        return jnp.concatenate([hi, lo, hi], axis=1)

    def rhs3(x, mask):
        hi, lo = split(x)
        hi = _stack_heads(hi, mask)
        return jnp.concatenate([hi, hi, _stack_heads(lo, mask)], axis=0)

    blk_t = si // RWKV_BASE
    blk_s = sj // RWKV_BASE
    diag8 = blk_t == blk_s
    tinv = {c: jnp.where(si == sj, 1.0, 0.0) + jnp.where(diag8, l_ab[c], 0.0) for c in chains}
    pw = {c: jnp.where(diag8, l_ab[c], 0.0) for c in chains}
    for _ in range(int(math.log2(RWKV_BASE)) - 1):
        for c in chains:
            pw[c] = _bdot(pw[c], _stack_heads(pw[c].astype(BF16), side_lo))
        for c in chains:
            tinv[c] = tinv[c] + _bdot(pw[c], _stack_heads(tinv[c].astype(BF16), side_lo))
    b = RWKV_BASE
    while b < C:
        lower_left = ((si // (2 * b)) == (sj // (2 * b))) & (((si // b) % 2) == 1) & (((sj // b) % 2) == 0)
        lt = {c: jnp.dot(lhs3(jnp.where(lower_left, l_ab[c], 0.0)), rhs3(tinv[c], side_lo),
                         preferred_element_type=F32) for c in chains}
        for c in chains:
            tinv[c] = tinv[c] + jnp.dot(lhs3(tinv[c]), rhs3(lt[c], side_lo), preferred_element_type=F32)
        b *= 2
    z = {c: x0[c][:C] + jnp.dot(a_ak[c], vst[c], preferred_element_type=F32) for c in chains}
    u = {c: jnp.dot(lhs3(tinv[c]), rhs3(z[c], lane_lo), preferred_element_type=F32) for c in chains}
    uv = {c: jnp.concatenate([_stack_heads(u[c].astype(BF16), lane_lo), vst[c]], axis=0) for c in chains}
    y_p = {c: x0[c][C:] + jnp.dot(a_r[c], uv[c], preferred_element_type=F32) for c in chains}
    for c in chains:
        n, j = c
        st_ref[n, j] = st_ref[n, j] * w_end[n][:, j * LANES:(j + 1) * LANES] + _bdot_tn(uv[c], bkw[c])
    y = jnp.concatenate([jnp.concatenate([y_p[(n, j)] for j in range(npair)], axis=1)
                         for n in range(NB)], axis=0)
    inv_n = 1.0 / RWKV_HEAD
    mean = _head_sum(y, hm) * inv_n
    yc = y - mean
    var = _head_sum(yc * yc, hm) * inv_n
    yn = yc * lax.rsqrt(var + RWKV_GN_EPS) * lng_ref[...] + lnb_ref[...]
    bonus = _head_sum(r * kmod * rk_ref[...], hm) * v
    out = ((yn + bonus) * g).astype(y_ref.dtype)
    for n in range(NB):
        y_ref[n] = out[n * C:(n + 1) * C]

    @pl.when(t == pl.num_programs(1) - 1)
    def _():
        sT_ref[...] = st_ref[...]


def _rwkv(p, shift_prev, s0, mu, w0, w_w2, a0, w_a2, w_g2, k_k, k_a, r_k, lnx_g, lnx_b):
    B, L, cols = p.shape
    H, N = s0.shape[1], s0.shape[2]
    W = H * N
    C = _time_tile(L, 64)
    assert 1 << int(math.log2(C)) == C and 2 * N == LANES
    npair = H // 2
    s0p = s0.reshape(B, npair, 2, N, N)
    z = jnp.zeros_like(s0p[:, :, 0])
    s0bd = jnp.concatenate([jnp.concatenate([s0p[:, :, 0], z], axis=-1),
                            jnp.concatenate([z, s0p[:, :, 1]], axis=-1)], axis=-2)
    lo = w_w2.shape[0]
    ww2p = jnp.concatenate([w_w2, jnp.zeros((LANES - lo, W), F32)], axis=0).astype(BF16)
    wa2p = jnp.concatenate([jnp.zeros((lo, W), F32), w_a2], axis=0).astype(BF16)
    hid = jnp.arange(LANES) // N
    hm = (hid[:, None] == hid[None, :]).astype(BF16)
    hm2 = jnp.concatenate([hm, hm], axis=0)
    nb = RWKV_SEQS if B % RWKV_SEQS == 0 else 1
    tr = jnp.arange(nb * C)
    tri = ((tr[:, None] >= tr[None, :]) & (tr[:, None] // C == tr[None, :] // C)).astype(BF16)
    row = lambda x: x.reshape(1, -1)
    y, sh, sT = pl.pallas_call(
        functools.partial(_rwkv_kernel, width=W),
        out_shape=[jax.ShapeDtypeStruct((B, L, W), BF16),
                   jax.ShapeDtypeStruct((B, 1, cols), F32),
                   jax.ShapeDtypeStruct((B, npair, LANES, LANES), F32)],
        grid=(B // nb, L // C),
        in_specs=[pl.BlockSpec((nb, C, cols), lambda b, t: (b, t, 0)),
                  pl.BlockSpec((nb, 1, cols), lambda b, t: (b, 0, 0)),
                  pl.BlockSpec((nb, npair, LANES, LANES), lambda b, t: (b, 0, 0, 0)),
                  _const_spec((1, cols)), _const_spec((1, W)), _const_spec((LANES, W)), _const_spec((1, W)),
                  _const_spec((LANES, W)), _const_spec((LANES, W)), _const_spec((1, W)), _const_spec((1, W)),
                  _const_spec((1, W)), _const_spec((1, W)), _const_spec((1, W)), _const_spec((2 * LANES, LANES)),
                  _const_spec((nb * C, 3 * nb * C))],
        out_specs=[pl.BlockSpec((nb, C, W), lambda b, t: (b, t, 0)),
                   pl.BlockSpec((nb, 1, cols), lambda b, t: (b, 0, 0)),
                   pl.BlockSpec((nb, npair, LANES, LANES), lambda b, t: (b, 0, 0, 0))],
        scratch_shapes=[pltpu.VMEM((nb, 1, cols), F32), pltpu.VMEM((nb, npair, LANES, LANES), F32)],
        compiler_params=_cparams(("parallel", "arbitrary")),
        name="rwkv7_mixer",
    )(p, shift_prev.reshape(B, 1, cols), s0bd, row(mu), row(w0), ww2p, row(a0), wa2p, w_g2.astype(BF16),
      row(k_k), row(k_a), row(r_k), row(lnx_g), row(lnx_b), hm2, _tile3(tri))
    sT = sT.reshape(B, npair, 2, N, 2, N)
    s_new = jnp.stack([sT[:, :, 0, :, 0, :], sT[:, :, 1, :, 1, :]], axis=2).reshape(B, H, N, N)
    return y, sh[:, 0], s_new


def _hgrn_segment_matrix(C):
    t = jnp.arange(C)[:, None]
    r = jnp.arange(C)[None, :]
    blocks = [r <= t]
    h = C // 2
    while h >= 1:
        mid = (t // (2 * h)) * (2 * h) + h - 1
        is_q = (t & h) != 0
        blocks.append(jnp.where(is_q, (r > mid) & (r <= t), (r > t) & (r <= mid)))
        h //= 2
    return jnp.concatenate(blocks, axis=0).astype(BF16)


def _hgrn_kernel(p_ref, s0_ref, llb_ref, l1lb_ref, omlb_ref, gn_ref, mseg_ref, y_ref, sT_ref, st_ref, *, width):
    t = pl.program_id(1)
    W = width
    H = W // HGRN_HEAD
    NB, C, _ = p_ref.shape

    @pl.when(t == 0)
    def _():
        for n in range(NB):
            for h in range(H):
                st_ref[n, h] = s0_ref[n, h].T

    p = jnp.concatenate([p_ref[n] for n in range(NB)], axis=0)
    q = _silu(p[:, :W])
    f = p[:, W:2 * W]
    v = p[:, 2 * W:3 * W]
    g = p[:, 3 * W:]
    ef = jnp.exp(-jnp.abs(f))
    lsig = jnp.minimum(f, 0.0) - jnp.log(1.0 + ef)
    c1 = llb_ref[...]
    c2 = l1lb_ref[...] + lsig
    log_f = jnp.maximum(c1, c2) + jnp.log(1.0 + jnp.exp(-jnp.abs(c1 - c2)))
    kv = omlb_ref[...] * (jnp.where(f >= 0.0, ef, 1.0) / (1.0 + ef))
    d_all = [_dot3(mseg_ref[...], log_f[n * C:(n + 1) * C]) for n in range(NB)]
    b = jnp.concatenate([d[:C] for d in d_all], axis=0)
    b_last = [d[C - 1:C, :] for d in d_all]
    qe = (q * jnp.exp(b)).astype(BF16)
    ke = (kv * jnp.exp(jnp.concatenate([jnp.broadcast_to(bl, (C, W)) for bl in b_last], axis=0) - b)).astype(BF16)
    e_last = [jnp.exp(bl) for bl in b_last]
    qk = q * kv
    vb = v.astype(BF16)
    rowi = lax.broadcasted_iota(jnp.int32, (NB * C, 1), 0)
    ti = lax.broadcasted_iota(jnp.int32, (C, C), 0)
    tj = lax.broadcasted_iota(jnp.int32, (C, C), 1)
    zs, masks = [], []
    hsz = C // 2
    lvl = 1
    while hsz >= 1:
        is_q = (rowi & hsz) != 0
        seg = jnp.concatenate([d[lvl * C:(lvl + 1) * C] for d in d_all], axis=0)
        zs.append((jnp.exp(seg) * jnp.where(is_q, q, kv)).astype(BF16))
        masks.append(((ti & hsz) != 0) & ((tj & hsz) == 0) & ((ti // (2 * hsz)) == (tj // (2 * hsz))))
        hsz //= 2
        lvl += 1
    chains = [(n, h) for n in range(NB) for h in range(H)]
    blk = lambda x, c: x[c[0] * C:(c[0] + 1) * C, c[1] * HGRN_HEAD:(c[1] + 1) * HGRN_HEAD]
    att = {}
    for z, mask in zip(zs, masks):
        for c in chains:
            zh = blk(z, c)
            term = jnp.where(mask, _bdot_nt(zh, zh), 0.0)
            att[c] = term if c not in att else att[c] + term
    o = {c: (_bdot(att[c], blk(vb, c)) + _bdot_nt(blk(qe, c), st_ref[c[0], c[1]])
             + jnp.sum(blk(qk, c), axis=-1, keepdims=True) * blk(v, c)) for c in chains}
    for c in chains:
        n, h = c
        st_ref[n, h] = (st_ref[n, h] * e_last[n][:, h * HGRN_HEAD:(h + 1) * HGRN_HEAD]
                        + _bdot_tn(blk(vb, c), blk(ke, c)))
    for c in chains:
        o[c] = o[c] * lax.rsqrt(jnp.mean(o[c] * o[c], axis=-1, keepdims=True) + NORM_EPS) * gn_ref[...]
    o = jnp.concatenate([jnp.concatenate([o[(n, h)] for h in range(H)], axis=1) for n in range(NB)], axis=0)
    out = (o * _silu(g)).astype(y_ref.dtype)
    for n in range(NB):
        y_ref[n] = out[n * C:(n + 1) * C]

    @pl.when(t == pl.num_programs(1) - 1)
    def _():
        for n in range(NB):
            for h in range(H):
                sT_ref[n, h] = st_ref[n, h].T


def _hgrn(p, s0, lb, gnorm):
    B, L, cols = p.shape
    W = cols // 4
    H = W // HGRN_HEAD
    C = _time_tile(L, 64)
    assert 1 << int(math.log2(C)) == C
    mseg = _tile3(_hgrn_segment_matrix(C))
    nb = HGRN_SEQS if B % HGRN_SEQS == 0 else 1
    row = lambda x: x.reshape(1, -1)
    y, sT = pl.pallas_call(
        functools.partial(_hgrn_kernel, width=W),
        out_shape=[jax.ShapeDtypeStruct((B, L, W), BF16),
                   jax.ShapeDtypeStruct((B, H, HGRN_HEAD, HGRN_HEAD), F32)],
        grid=(B // nb, L // C),
        in_specs=[pl.BlockSpec((nb, C, cols), lambda b, t: (b, t, 0)),
                  pl.BlockSpec((nb, H, HGRN_HEAD, HGRN_HEAD), lambda b, t: (b, 0, 0, 0)),
                  _const_spec((1, W)), _const_spec((1, W)), _const_spec((1, W)), _const_spec((1, HGRN_HEAD)),
                  _const_spec(mseg.shape)],
        out_specs=[pl.BlockSpec((nb, C, W), lambda b, t: (b, t, 0)),
                   pl.BlockSpec((nb, H, HGRN_HEAD, HGRN_HEAD), lambda b, t: (b, 0, 0, 0))],
        scratch_shapes=[pltpu.VMEM((nb, H, HGRN_HEAD, HGRN_HEAD), F32)],
        compiler_params=_cparams(("parallel", "arbitrary")),
        name="hgrn2_mixer",
    )(p, s0, row(jnp.log(lb)), row(jnp.log1p(-lb)), row(1.0 - lb), row(gnorm), mseg)
    return y, sT


def _trunk(x, mods_all, s5_re, s5_im, rwkv_s, rwkv_shift, hgrn_s, P, Pb):
    depth = P['norm_mix'].shape[0]
    lb_all = jax.nn.softmax(P['hgrn_lower_bounds'], axis=0)
    lb_all = jnp.cumsum(lb_all, axis=0) - lb_all[0]
    s5_width = P['s5_w_glu'].shape[1]
    n_re, n_im, n_rw, n_sh, n_hg = [], [], [], [], []
    B, L, D = x.shape
    flat = L < FLAT_BELOW
    if flat:
        x = x.reshape(1, B * L, D)
        mods_all = jnp.repeat(mods_all, L, axis=1).transpose(0, 2, 1, 3)[:, None]
    else:
        mods_all = mods_all[:, :, :, None, :]
    seq = lambda a: a.reshape(B, L, a.shape[-1])
    tok = lambda a: a.reshape(x.shape[0], x.shape[1], a.shape[-1])
    for l in range(depth):
        mods = mods_all[l]
        x = _ffn(x, mods, P['norm_ffn1'][l], Pb['ffn1_w1'][l], Pb['ffn1_w3'][l], Pb['ffn1_w2'][l], 0)
        i = l // 2
        if l % 2 == 0:
            n_in = Pb['ab_w_in'].shape[2]
            u, pr = _inproj(x, mods, P['norm_mix'][l], Pb['ab_w_in'][i], (s5_width, n_in - s5_width))
            ya, hr, hi = _s5(seq(u), s5_re[i], s5_im[i], P['s5_lam_re'][i], P['s5_lam_im'][i],
                             P['s5_log_dt'][i], P['s5_b_re'][i], P['s5_b_im'][i], P['s5_c_re'][i],
                             P['s5_c_im'][i], P['s5_d'][i], Pb['s5_w_glu'][i])
            yb, sh_new, rw_new = _rwkv(seq(pr), rwkv_shift[i], rwkv_s[i], P['rwkv_mu'][i], P['rwkv_w0'][i],
                                       P['rwkv_w_w2'][i], P['rwkv_a0'][i], P['rwkv_w_a2'][i],
                                       P['rwkv_w_g2'][i], P['rwkv_k_k'][i], P['rwkv_k_a'][i],
                                       P['rwkv_r_k'][i], P['rwkv_lnx_g'][i], P['rwkv_lnx_b'][i])
            ys, w_out = (tok(ya), tok(yb)), Pb['ab_w_out'][i]
            n_re.append(hr)
            n_im.append(hi)
            n_rw.append(rw_new)
            n_sh.append(sh_new)
        else:
            (pc,) = _inproj(x, mods, P['norm_mix'][l], Pb['c_w_in'][i], (Pb['c_w_in'].shape[2],))
            yc, hg_new = _hgrn(seq(pc), hgrn_s[i], lb_all[l], P['hgrn_gnorm'][i])
            ys, w_out = (tok(yc),), Pb['c_w_out'][i]
            n_hg.append(hg_new)
        fin = P['final_norm'] if l == depth - 1 else None
        x = _ffn(x, mods, P['norm_ffn2'][l], Pb['ffn2_w1'][l], Pb['ffn2_w3'][l], Pb['ffn2_w2'][l], 2,
                 final_gain=fin, ys=ys, w_out=w_out)
    return (x.reshape(B, L, D), jnp.stack(n_re), jnp.stack(n_im), jnp.stack(n_rw), jnp.stack(n_sh),
            jnp.stack(n_hg))


def kernel(x_prompt, x_sample, state_s5_re, state_s5_im, state_rwkv, state_rwkv_shift, state_hgrn, c_prompt, c_sample, w_ada, b_ada, norm_ffn1, norm_mix, norm_ffn2, ffn1_w1, ffn1_w3, ffn1_w2, ffn2_w1, ffn2_w3, ffn2_w2, ab_w_in, ab_w_out, s5_lam_re, s5_lam_im, s5_log_dt, s5_b_re, s5_b_im, s5_c_re, s5_c_im, s5_d, s5_w_glu, rwkv_mu, rwkv_w0, rwkv_w_w2, rwkv_a0, rwkv_w_a2, rwkv_w_g2, rwkv_k_k, rwkv_k_a, rwkv_r_k, rwkv_lnx_g, rwkv_lnx_b, c_w_in, c_w_out, hgrn_lower_bounds, hgrn_gnorm, final_norm):
    P = dict(norm_ffn1=norm_ffn1, norm_mix=norm_mix, norm_ffn2=norm_ffn2,
             s5_lam_re=s5_lam_re, s5_lam_im=s5_lam_im, s5_log_dt=s5_log_dt, s5_b_re=s5_b_re, s5_b_im=s5_b_im,
             s5_c_re=s5_c_re, s5_c_im=s5_c_im, s5_d=s5_d, s5_w_glu=s5_w_glu, rwkv_mu=rwkv_mu,
             rwkv_w0=rwkv_w0, rwkv_w_w2=rwkv_w_w2, rwkv_a0=rwkv_a0, rwkv_w_a2=rwkv_w_a2, rwkv_w_g2=rwkv_w_g2,
             rwkv_k_k=rwkv_k_k, rwkv_k_a=rwkv_k_a, rwkv_r_k=rwkv_r_k, rwkv_lnx_g=rwkv_lnx_g,
             rwkv_lnx_b=rwkv_lnx_b, hgrn_lower_bounds=hgrn_lower_bounds, hgrn_gnorm=hgrn_gnorm,
             final_norm=final_norm)
    Pb = {n: w.astype(BF16) for n, w in dict(
        ffn1_w1=ffn1_w1, ffn1_w3=ffn1_w3, ffn1_w2=ffn1_w2, ffn2_w1=ffn2_w1, ffn2_w3=ffn2_w3, ffn2_w2=ffn2_w2,
        ab_w_in=ab_w_in, ab_w_out=ab_w_out, c_w_in=c_w_in, c_w_out=c_w_out, s5_w_glu=s5_w_glu).items()}
    depth, d = norm_mix.shape
    bp, bs = x_prompt.shape[0], x_sample.shape[0]
    mods = _ada(jnp.concatenate([c_prompt, c_sample], axis=0), w_ada, b_ada)
    mods = mods.reshape(depth, bp + bs, N_MOD, d)
    dt = x_prompt.dtype
    zeros_like_b = lambda s: jnp.zeros((s.shape[0], bp) + s.shape[2:], dt)
    outs_p = _trunk(x_prompt, mods[:, :bp], zeros_like_b(state_s5_re), zeros_like_b(state_s5_im),
                    zeros_like_b(state_rwkv), zeros_like_b(state_rwkv_shift), zeros_like_b(state_hgrn), P, Pb)
    outs_s = _trunk(x_sample, mods[:, bp:], state_s5_re, state_s5_im, state_rwkv, state_rwkv_shift,
                    state_hgrn, P, Pb)
    return (outs_p[0], outs_s[0]) + tuple(outs_p[1:]) + tuple(outs_s[1:])
```

```python
import functools
import math

import jax
import jax.numpy as jnp
from jax import lax
from jax.experimental import pallas as pl
from jax.experimental.pallas import tpu as pltpu

F32 = jnp.float32
BF16 = jnp.bfloat16

NORM_EPS = 1e-6
LOG2E = 1.4426950408889634
RWKV_GN_EPS = 64e-5
N_MOD = 9
RWKV_HEAD = 64
RWKV_BASE = 8
HGRN_HEAD = 128
LANES = 128
S5_SEQS = 8
S5_STEPS = 64
RWKV_SEQS = 4
HGRN_SEQS = 4
VMEM_LIMIT = 56 * 1024 * 1024
FFN_ROWS = 1024
FLAT_BELOW = 64


def _bdot(a, b):
    return jnp.dot(a.astype(BF16), b.astype(BF16), preferred_element_type=F32)


def _bdot_nt(a, b):
    return lax.dot_general(a.astype(BF16), b.astype(BF16), (((1,), (1,)), ((), ())),
                           preferred_element_type=F32)


def _bdot_tn(a, b):
    return lax.dot_general(a.astype(BF16), b.astype(BF16), (((0,), (0,)), ((), ())),
                           preferred_element_type=F32)


def _dot2(a, b2):
    hi = a.astype(BF16)
    lo = (a - hi.astype(F32)).astype(BF16)
    return jnp.dot(jnp.concatenate([hi, lo], axis=1), b2, preferred_element_type=F32)


def _dot3(m3, x):
    hi = x.astype(BF16)
    r1 = x - hi.astype(F32)
    mid = r1.astype(BF16)
    lo = (r1 - mid.astype(F32)).astype(BF16)
    return jnp.dot(m3, jnp.concatenate([hi, mid, lo], axis=0), preferred_element_type=F32)


def _tile3(m):
    return jnp.concatenate([m, m, m], axis=1)


def _sigmoid(x):
    return 1.0 / (1.0 + jnp.exp(-x))


def _silu(x):
    return x * _sigmoid(x)


def _softplus(x):
    return jnp.maximum(x, 0.0) + jnp.log(1.0 + jnp.exp(-jnp.abs(x)))


def _cparams(sem):
    return pltpu.CompilerParams(dimension_semantics=sem, vmem_limit_bytes=VMEM_LIMIT)


def _const_spec(shape):
    nd = len(shape)
    return pl.BlockSpec(shape, lambda *_: (0,) * nd, pipeline_mode=pl.Buffered(1))


def _time_tile(L, want):
    t = min(L, want)
    assert L % t == 0 and t % 8 == 0
    return t


def _ada_kernel(c_ref, w_ref, b_ref, o_ref):
    cs = _silu(c_ref[...])
    o_ref[0] = _bdot(cs, w_ref[0]) + b_ref[0]


def _ada(c_all, w_ada, b_ada):
    depth, d, n = w_ada.shape
    nb = c_all.shape[0]
    tn = 1024
    return pl.pallas_call(
        _ada_kernel,
        out_shape=jax.ShapeDtypeStruct((depth, nb, n), F32),
        grid=(depth, n // tn),
        in_specs=[pl.BlockSpec((nb, d), lambda l, j: (0, 0)),
                  pl.BlockSpec((1, d, tn), lambda l, j: (l, 0, j)),
                  pl.BlockSpec((1, 1, tn), lambda l, j: (l, 0, j))],
        out_specs=pl.BlockSpec((1, nb, tn), lambda l, j: (l, 0, j)),
        compiler_params=_cparams(("arbitrary", "arbitrary")),
        name="ada_mod",
    )(c_all, w_ada, b_ada.reshape(depth, 1, n))


def _norm_mod(x, gain, shift, scale):
    y = x * lax.rsqrt(jnp.mean(x * x, axis=-1, keepdims=True) + NORM_EPS) * gain
    return y * (1.0 + scale) + shift


def _ffn_kernel(x_ref, mod_ref, g_ref, w1_ref, w3_ref, w2_ref, *rest, which, fc, final, n_y):
    y_refs = rest[:n_y]
    rest = rest[n_y:]
    if n_y:
        wo_ref, rest = rest[0], rest[1:]
    if final:
        gf_ref, o_ref = rest
    else:
        (o_ref,) = rest
    x = x_ref[0]
    if n_y:
        off = 0
        mix = None
        for y_ref in y_refs:
            k = y_ref.shape[2]
            part = jnp.dot(y_ref[0], wo_ref[off:off + k, :], preferred_element_type=F32)
            mix = part if mix is None else mix + part
            off += k
        x = x + (1.0 + mod_ref[0, 5]) * mix
    shift = mod_ref[0, 3 * which]
    scale = mod_ref[0, 3 * which + 1]
    gate = mod_ref[0, 3 * which + 2]
    h = _norm_mod(x, g_ref[...], shift, scale).astype(BF16)
    d_ff = w1_ref.shape[1]
    acc = jnp.zeros(x.shape, F32)
    for c in range(d_ff // fc):
        a = jnp.dot(h, w1_ref[:, c * fc:(c + 1) * fc], preferred_element_type=F32)
        b = jnp.dot(h, w3_ref[:, c * fc:(c + 1) * fc], preferred_element_type=F32)
        u = (_silu(a) * b).astype(BF16)
        acc = acc + jnp.dot(u, w2_ref[c * fc:(c + 1) * fc, :], preferred_element_type=F32)
    y = x + 0.5 * (1.0 + gate) * acc
    if final:
        y = y * lax.rsqrt(jnp.mean(y * y, axis=-1, keepdims=True) + NORM_EPS) * gf_ref[...]
    o_ref[0] = y


def _mod_spec(mods, tm):
    _, n_mod, R, D = mods.shape
    if R == 1:
        return pl.BlockSpec((1, n_mod, 1, D), lambda b, i: (b, 0, 0, 0))
    return pl.BlockSpec((1, n_mod, tm, D), lambda b, i: (b, 0, i, 0))


def _layer_spec(shape, layer):
    nd = len(shape) - 1
    return pl.BlockSpec((None,) + tuple(shape[1:]), lambda *_: (layer,) + (0,) * nd, pipeline_mode=pl.Buffered(1))


def _ffn(x, mods, gain, w1, w3, w2, layer, which, final_gain=None, ys=(), w_out=None):
    B, L, D = x.shape
    d_ff = w1.shape[2]
    tm = _time_tile(L, FFN_ROWS)
    fc = 256
    assert d_ff % fc == 0
    final = final_gain is not None
    ins = [x, mods, gain.reshape(1, D), w1, w3, w2]
    specs = [pl.BlockSpec((1, tm, D), lambda b, i: (b, i, 0)), _mod_spec(mods, tm), _const_spec((1, D)),
             _layer_spec(w1.shape, layer), _layer_spec(w3.shape, layer), _layer_spec(w2.shape, layer)]
    if ys:
        assert sum(y.shape[2] for y in ys) == w_out.shape[0]
        ins += list(ys) + [w_out]
        specs += [pl.BlockSpec((1, tm, y.shape[2]), lambda b, i: (b, i, 0)) for y in ys]
        specs.append(_const_spec(w_out.shape))
    if final:
        ins.append(final_gain.reshape(1, D))
        specs.append(_const_spec((1, D)))
    return pl.pallas_call(
        functools.partial(_ffn_kernel, which=which, fc=fc, final=final, n_y=len(ys)),
        out_shape=jax.ShapeDtypeStruct((B, L, D), F32),
        grid=(B, L // tm),
        in_specs=specs,
        out_specs=pl.BlockSpec((1, tm, D), lambda b, i: (b, i, 0)),
        compiler_params=_cparams(("parallel", "parallel")),
        name="ffn",
    )(*ins)


def _inproj_kernel(x_ref, mod_ref, g_ref, w_ref, *o_refs, splits):
    x = x_ref[0]
    h = _norm_mod(x, g_ref[...], mod_ref[0, 3], mod_ref[0, 4]).astype(BF16)
    off = 0
    for o_ref, n in zip(o_refs, splits):
        o_ref[0] = jnp.dot(h, w_ref[:, off:off + n], preferred_element_type=F32)
        off += n


def _inproj(x, mods, gain, w_in, splits):
    B, L, D = x.shape
    n = w_in.shape[1]
    assert sum(splits) == n
    tm = _time_tile(L, FFN_ROWS)
    return pl.pallas_call(
        functools.partial(_inproj_kernel, splits=splits),
        out_shape=[jax.ShapeDtypeStruct((B, L, s), F32) for s in splits],
        grid=(B, L // tm),
        in_specs=[pl.BlockSpec((1, tm, D), lambda b, i: (b, i, 0)), _mod_spec(mods, tm),
                  _const_spec((1, D)), _const_spec((D, n))],
        out_specs=[pl.BlockSpec((1, tm, s), lambda b, i: (b, i, 0)) for s in splits],
        compiler_params=_cparams(("parallel", "parallel")),
        name="mixer_in_proj",
    )(x, mods, gain.reshape(1, D), w_in)


def _s5_kernel(u_ref, h0_ref, perm_ref, permt_ref, wb_ref, wc_ref, a1r_ref, a1i_ref, d_ref, wg_ref,
               y_ref, hT_ref, carry_ref, *, tseg):
    t = pl.program_id(1)

    @pl.when(t == 0)
    def _():
        carry_ref[...] = h0_ref[...]

    u = jnp.concatenate([u_ref[k] for k in range(S5_SEQS)], axis=0)
    u = _dot3(perm_ref[...], u)
    nblk = u.shape[1] // LANES
    half = wb_ref.shape[2] // 2
    ys = []
    for j in range(nblk):
        cols = slice(j * half, (j + 1) * half)
        re_cols = slice(2 * j * half, (2 * j + 1) * half)
        im_cols = slice((2 * j + 1) * half, (2 * j + 2) * half)
        bu = _bdot(u[:, j * LANES:(j + 1) * LANES], wb_ref[j])
        ar = a1r_ref[:, cols]
        ai = a1i_ref[:, cols]
        sr = carry_ref[:, re_cols]
        si = carry_ref[:, im_cols]
        xr_rows, xi_rows = [], []
        for i in range(tseg):
            br = bu[S5_SEQS * i:S5_SEQS * (i + 1), :half]
            bi = bu[S5_SEQS * i:S5_SEQS * (i + 1), half:]
            sr, si = ar * sr - ai * si + br, ar * si + ai * sr + bi
            xr_rows.append(sr)
            xi_rows.append(si)
        carry_ref[:, re_cols] = sr
        carry_ref[:, im_cols] = si
        xr = jnp.concatenate(xr_rows, axis=0)
        xi = jnp.concatenate(xi_rows, axis=0)
        ys.append(_bdot(xr, wc_ref[j, :half, :]) + _bdot(xi, wc_ref[j, half:, :]))
    y = jnp.concatenate(ys, axis=1) + u * d_ref[...]
    y = jax.nn.gelu(y)
    y = (y * _sigmoid(_bdot(y, wg_ref[...]))).astype(BF16)
    y = jnp.dot(permt_ref[...], y, preferred_element_type=F32).astype(y_ref.dtype)
    for k in range(S5_SEQS):
        y_ref[k] = y[k * tseg:(k + 1) * tseg]
    hT_ref[...] = carry_ref[...]


def _s5_tables(lam_re, lam_im, log_dt, b_re, b_im, c_re, c_im):
    G, P = lam_re.shape
    H = b_re.shape[2]
    gpb = LANES // H
    nblk = G // gpb
    dt = jnp.exp(log_dt)[:, None]
    mag = jnp.exp(lam_re * dt)
    ab_re = mag * jnp.cos(lam_im * dt)
    ab_im = mag * jnp.sin(lam_im * dt)
    den = lam_re * lam_re + lam_im * lam_im
    z_re = ((ab_re - 1.0) * lam_re + ab_im * lam_im) / den
    z_im = (ab_im * lam_re - (ab_re - 1.0) * lam_im) / den
    bb_re = z_re[..., None] * b_re - z_im[..., None] * b_im
    bb_im = z_re[..., None] * b_im + z_im[..., None] * b_re
    eye = jnp.eye(gpb, dtype=F32)

    def blockdiag_in(bb):
        bb = bb.reshape(nblk, gpb, P, H)
        m = jnp.einsum('jgph,gk->jghkp', bb, eye)
        return m.reshape(nblk, gpb * H, gpb * P)

    def blockdiag_out(cc):
        cc = cc.reshape(nblk, gpb, H, P)
        m = jnp.einsum('jghp,gk->jgpkh', cc, eye)
        return m.reshape(nblk, gpb * P, gpb * H)

    wb = jnp.concatenate([blockdiag_in(bb_re), blockdiag_in(bb_im)], axis=2).astype(BF16)
    wc = jnp.concatenate([blockdiag_out(c_re), blockdiag_out(-c_im)], axis=1).astype(BF16)

    a1r = jnp.broadcast_to(ab_re.reshape(1, G * P), (S5_SEQS, G * P))
    a1i = jnp.broadcast_to(ab_im.reshape(1, G * P), (S5_SEQS, G * P))
    return wb, wc, a1r, a1i


def _s5_state_to_cols(h_re, h_im, nblk):
    B, G, P = h_re.shape
    gpb = G // nblk
    st = jnp.stack([h_re.reshape(B, nblk, gpb * P), h_im.reshape(B, nblk, gpb * P)], axis=2)
    return st.reshape(B, 2 * G * P)


def _s5_cols_to_state(cols, G, P, nblk):
    B = cols.shape[0]
    st = cols.reshape(B, nblk, 2, (G // nblk) * P)
    return st[:, :, 0].reshape(B, G, P), st[:, :, 1].reshape(B, G, P)


def _s5(u, h_re, h_im, lam_re, lam_im, log_dt, b_re, b_im, c_re, c_im, d_skip, w_glu_bf16):
    B, L, W = u.shape
    G, P = lam_re.shape
    assert B % S5_SEQS == 0
    tseg = _time_tile(L, S5_STEPS)
    T = S5_SEQS * tseg
    nblk = W // LANES
    tables = _s5_tables(lam_re, lam_im, log_dt, b_re, b_im, c_re, c_im)
    r = jnp.arange(T)
    perm = ((r % S5_SEQS) * tseg + r // S5_SEQS)[:, None] == jnp.arange(T)[None, :]
    perm = perm.astype(BF16)
    h0 = _s5_state_to_cols(h_re, h_im, nblk)
    ncol = 2 * G * P
    consts = (_tile3(perm), perm.T) + tables + (d_skip.reshape(1, W), w_glu_bf16)
    y, hT = pl.pallas_call(
        functools.partial(_s5_kernel, tseg=tseg),
        out_shape=[jax.ShapeDtypeStruct((B, L, W), BF16), jax.ShapeDtypeStruct((B, ncol), F32)],
        grid=(B // S5_SEQS, L // tseg),
        in_specs=[pl.BlockSpec((S5_SEQS, tseg, W), lambda b, t: (b, t, 0)),
                  pl.BlockSpec((S5_SEQS, ncol), lambda b, t: (b, 0))]
        + [_const_spec(c.shape) for c in consts],
        out_specs=[pl.BlockSpec((S5_SEQS, tseg, W), lambda b, t: (b, t, 0)),
                   pl.BlockSpec((S5_SEQS, ncol), lambda b, t: (b, 0))],
        scratch_shapes=[pltpu.VMEM((S5_SEQS, ncol), F32)],
        compiler_params=_cparams(("parallel", "arbitrary")),
        name="s5_mixer",
    )(u, h0, *consts)
    hr, hi = _s5_cols_to_state(hT, G, P, nblk)
    return y, hr, hi


def _stack_heads(x, lane_lo):
    zero = jnp.zeros_like(x)
    return jnp.concatenate([jnp.where(lane_lo, x, zero), jnp.where(lane_lo, zero, x)], axis=0)


def _head_sum(x, hm2):
    nblk = x.shape[1] // LANES
    R = x.shape[0]
    xs = jnp.concatenate([x[:, j * LANES:(j + 1) * LANES] for j in range(nblk)], axis=0)
    s = _dot2(xs, hm2)
    return jnp.concatenate([s[j * R:(j + 1) * R] for j in range(nblk)], axis=1)


def _rwkv_kernel(p_ref, sp_ref, s0_ref, mu_ref, w0_ref, ww2_ref, a0_ref, wa2_ref, wg2_ref, kk_ref, ka_ref,
                 rk_ref, lng_ref, lnb_ref, hm_ref, tri_ref, y_ref, sh_ref, sT_ref, prev_ref, st_ref, *, width):
    t = pl.program_id(1)
    W = width
    NB, C, _ = p_ref.shape

    @pl.when(t == 0)
    def _():
        prev_ref[...] = sp_ref[...]
        st_ref[...] = s0_ref[...]

    rows = lax.broadcasted_iota(jnp.int32, (C, 1), 0)
    ps = []
    for n in range(NB):
        p = p_ref[n]
        p_prev = jnp.where(rows == 0, prev_ref[n], pltpu.roll(p, 1, axis=0))
        prev_ref[n] = p[C - 1:C, :]
        sh_ref[n] = p[C - 1:C, :]
        ps.append(p + (p_prev - p) * mu_ref[...])
    ps = jnp.concatenate(ps, axis=0)
    r = ps[:, :W]
    k = ps[:, W:2 * W]
    v = ps[:, 2 * W:3 * W]
    wa_low = ps[:, 3 * W:3 * W + LANES]
    g_low = ps[:, 3 * W + LANES:]
    xw = w0_ref[...] + _bdot(jnp.tanh(wa_low), ww2_ref[...])
    lw = -LOG2E * jnp.exp(-_softplus(-xw) - 0.5)
    a = _sigmoid(a0_ref[...] + _bdot(wa_low, wa2_ref[...]))
    g = _bdot(_sigmoid(g_low), wg2_ref[...])
    hm = hm_ref[...]
    kk = k * kk_ref[...]
    kk = kk / jnp.maximum(jnp.sqrt(_head_sum(kk * kk, hm)), 1e-12)
    kmod = k * (1.0 + (a - 1.0) * ka_ref[...])
    bvec = kk * a

    cum = _dot3(tri_ref[...], lw)
    e_in = jnp.exp2(cum)
    e_ex = jnp.exp2(cum - lw)
    e_neg = jnp.exp2(-cum)
    cum_last = [cum[(n + 1) * C - 1:(n + 1) * C, :] for n in range(NB)]
    e_end = jnp.exp2(jnp.concatenate([jnp.broadcast_to(c, (C, W)) for c in cum_last], axis=0) - cum)
    w_end = [jnp.exp2(c) for c in cum_last]
    at = (-kk * e_ex).astype(BF16)
    rt = (r * e_in).astype(BF16)
    bt = (bvec * e_neg).astype(BF16)
    kt = (kmod * e_neg).astype(BF16)
    bw = (bvec * e_end).astype(BF16)
    kw = (kmod * e_end).astype(BF16)
    vb = v.astype(BF16)

    C2 = 2 * C
    si = lax.broadcasted_iota(jnp.int32, (C, C2), 0)
    sj = lax.broadcasted_iota(jnp.int32, (C, C2), 1) % C
    strict = si > sj
    lower = si >= sj
    lane_lo = lax.broadcasted_iota(jnp.int32, (1, LANES), 1) < RWKV_HEAD
    side_lo = lax.broadcasted_iota(jnp.int32, (1, C2), 1) < C
    npair = W // LANES
    chains = [(n, j) for n in range(NB) for j in range(npair)]
    x0, vst, bkw, l_ab, a_ak, a_r = {}, {}, {}, {}, {}, {}
    for c in chains:
        n, j = c
        pair = lambda x: x[n * C:(n + 1) * C, j * LANES:(j + 1) * LANES]
        ar = jnp.concatenate([pair(at), pair(rt)], axis=0)
        bk_st = jnp.concatenate([_stack_heads(pair(bt), lane_lo), _stack_heads(pair(kt), lane_lo)], axis=0)
        m = _bdot_nt(ar, bk_st)
        x0[c] = _bdot_nt(ar, st_ref[n, j])
        vst[c] = _stack_heads(pair(vb), lane_lo)
        bkw[c] = jnp.concatenate([_stack_heads(pair(bw), lane_lo), _stack_heads(pair(kw), lane_lo)], axis=0)
        l_ab[c] = jnp.where(strict, m[:C, :C2], 0.0)
        a_ak[c] = jnp.where(strict, m[:C, C2:], 0.0).astype(BF16)
        a_r[c] = jnp.concatenate([jnp.where(lower, m[C:, :C2], 0.0), jnp.where(lower, m[C:, C2:], 0.0)],
                                 axis=1).astype(BF16)
    def split(x):
        hi = x.astype(BF16)
        return hi, (x - hi.astype(F32)).astype(BF16)

    def lhs3(x):
        hi, lo = split(x)
        return jnp.concatenate([hi, lo, hi], axis=1)

    def rhs3(x, mask):
        hi, lo = split(x)
        hi = _stack_heads(hi, mask)
        return jnp.concatenate([hi, hi, _stack_heads(lo, mask)], axis=0)

    blk_t = si // RWKV_BASE
    blk_s = sj // RWKV_BASE
    diag8 = blk_t == blk_s
    l_q = {c: l_ab[c].astype(BF16) for c in chains}
    zero = jnp.zeros((C, C2), BF16)
    tinv = {c: jnp.where(si == sj, 1.0, 0.0) + jnp.where(diag8, l_q[c], zero).astype(F32) for c in chains}
    pw = {c: jnp.where(diag8, l_q[c], zero) for c in chains}
    for _ in range(int(math.log2(RWKV_BASE)) - 1):
        for c in chains:
            pw[c] = _bdot(pw[c], _stack_heads(pw[c].astype(BF16), side_lo))
        for c in chains:
            tinv[c] = tinv[c] + _bdot(pw[c], _stack_heads(tinv[c].astype(BF16), side_lo))

    def rhs2(x, mask):
        hi, lo = split(x)
        return jnp.concatenate([_stack_heads(hi, mask), _stack_heads(lo, mask)], axis=0)

    b = RWKV_BASE
    while b < C:
        lower_left = ((si // (2 * b)) == (sj // (2 * b))) & (((si // b) % 2) == 1) & (((sj // b) % 2) == 0)
        l_b = {c: jnp.where(lower_left, l_q[c], zero) for c in chains}
        lt = {c: jnp.dot(jnp.concatenate([l_b[c], l_b[c]], axis=1), rhs2(tinv[c], side_lo),
                         preferred_element_type=F32) for c in chains}
        for c in chains:
            tinv[c] = tinv[c] + jnp.dot(lhs3(tinv[c]), rhs3(lt[c], side_lo), preferred_element_type=F32)
        b *= 2
    z = {c: x0[c][:C] + jnp.dot(a_ak[c], vst[c], preferred_element_type=F32) for c in chains}
    u = {c: jnp.dot(lhs3(tinv[c]), rhs3(z[c], lane_lo), preferred_element_type=F32) for c in chains}
    uv = {c: jnp.concatenate([_stack_heads(u[c].astype(BF16), lane_lo), vst[c]], axis=0) for c in chains}
    y_p = {c: x0[c][C:] + jnp.dot(a_r[c], uv[c], preferred_element_type=F32) for c in chains}
    for c in chains:
        n, j = c
        st_ref[n, j] = st_ref[n, j] * w_end[n][:, j * LANES:(j + 1) * LANES] + _bdot_tn(uv[c], bkw[c])
    y = jnp.concatenate([jnp.concatenate([y_p[(n, j)] for j in range(npair)], axis=1)
                         for n in range(NB)], axis=0)
    inv_n = 1.0 / RWKV_HEAD
    mean = _head_sum(y, hm) * inv_n
    yc = y - mean
    var = _head_sum(yc * yc, hm) * inv_n
    yn = yc * lax.rsqrt(var + RWKV_GN_EPS) * lng_ref[...] + lnb_ref[...]
    bonus = _head_sum(r * kmod * rk_ref[...], hm) * v
    out = ((yn + bonus) * g).astype(y_ref.dtype)
    for n in range(NB):
        y_ref[n] = out[n * C:(n + 1) * C]

    @pl.when(t == pl.num_programs(1) - 1)
    def _():
        sT_ref[...] = st_ref[...]


def _rwkv(p, shift_prev, s0, mu, w0, w_w2, a0, w_a2, w_g2, k_k, k_a, r_k, lnx_g, lnx_b):
    B, L, cols = p.shape
    H, N = s0.shape[1], s0.shape[2]
    W = H * N
    C = _time_tile(L, 64)
    assert 1 << int(math.log2(C)) == C and 2 * N == LANES
    npair = H // 2
    s0p = s0.reshape(B, npair, 2, N, N)
    z = jnp.zeros_like(s0p[:, :, 0])
    s0bd = jnp.concatenate([jnp.concatenate([s0p[:, :, 0], z], axis=-1),
                            jnp.concatenate([z, s0p[:, :, 1]], axis=-1)], axis=-2)
    lo = w_w2.shape[0]
    ww2p = jnp.concatenate([w_w2, jnp.zeros((LANES - lo, W), F32)], axis=0).astype(BF16)
    wa2p = jnp.concatenate([jnp.zeros((lo, W), F32), w_a2], axis=0).astype(BF16)
    hid = jnp.arange(LANES) // N
    hm = (hid[:, None] == hid[None, :]).astype(BF16)
    hm2 = jnp.concatenate([hm, hm], axis=0)
    nb = RWKV_SEQS if B % RWKV_SEQS == 0 else 1
    tr = jnp.arange(nb * C)
    tri = ((tr[:, None] >= tr[None, :]) & (tr[:, None] // C == tr[None, :] // C)).astype(BF16)
    row = lambda x: x.reshape(1, -1)
    y, sh, sT = pl.pallas_call(
        functools.partial(_rwkv_kernel, width=W),
        out_shape=[jax.ShapeDtypeStruct((B, L, W), BF16),
                   jax.ShapeDtypeStruct((B, 1, cols), F32),
                   jax.ShapeDtypeStruct((B, npair, LANES, LANES), F32)],
        grid=(B // nb, L // C),
        in_specs=[pl.BlockSpec((nb, C, cols), lambda b, t: (b, t, 0)),
                  pl.BlockSpec((nb, 1, cols), lambda b, t: (b, 0, 0)),
                  pl.BlockSpec((nb, npair, LANES, LANES), lambda b, t: (b, 0, 0, 0)),
                  _const_spec((1, cols)), _const_spec((1, W)), _const_spec((LANES, W)), _const_spec((1, W)),
                  _const_spec((LANES, W)), _const_spec((LANES, W)), _const_spec((1, W)), _const_spec((1, W)),
                  _const_spec((1, W)), _const_spec((1, W)), _const_spec((1, W)), _const_spec((2 * LANES, LANES)),
                  _const_spec((nb * C, 3 * nb * C))],
        out_specs=[pl.BlockSpec((nb, C, W), lambda b, t: (b, t, 0)),
                   pl.BlockSpec((nb, 1, cols), lambda b, t: (b, 0, 0)),
                   pl.BlockSpec((nb, npair, LANES, LANES), lambda b, t: (b, 0, 0, 0))],
        scratch_shapes=[pltpu.VMEM((nb, 1, cols), F32), pltpu.VMEM((nb, npair, LANES, LANES), F32)],
        compiler_params=_cparams(("parallel", "arbitrary")),
        name="rwkv7_mixer",
    )(p, shift_prev.reshape(B, 1, cols), s0bd, row(mu), row(w0), ww2p, row(a0), wa2p, w_g2.astype(BF16),
      row(k_k), row(k_a), row(r_k), row(lnx_g), row(lnx_b), hm2, _tile3(tri))
    sT = sT.reshape(B, npair, 2, N, 2, N)
    s_new = jnp.stack([sT[:, :, 0, :, 0, :], sT[:, :, 1, :, 1, :]], axis=2).reshape(B, H, N, N)
    return y, sh[:, 0], s_new


def _hgrn_segment_matrix(C):
    t = jnp.arange(C)[:, None]
    r = jnp.arange(C)[None, :]
    blocks = [r <= t]
    h = C // 2
    while h >= 1:
        mid = (t // (2 * h)) * (2 * h) + h - 1
        is_q = (t & h) != 0
        blocks.append(jnp.where(is_q, (r > mid) & (r <= t), (r > t) & (r <= mid)))
        h //= 2
    return jnp.concatenate(blocks, axis=0).astype(BF16)


def _hgrn_kernel(p_ref, s0_ref, llb_ref, l1lb_ref, omlb_ref, gn_ref, mseg_ref, y_ref, sT_ref, st_ref, *, width):
    t = pl.program_id(1)
    W = width
    H = W // HGRN_HEAD
    NB, C, _ = p_ref.shape

    @pl.when(t == 0)
    def _():
        for n in range(NB):
            for h in range(H):
                st_ref[n, h] = s0_ref[n, h].T

    p = jnp.concatenate([p_ref[n] for n in range(NB)], axis=0)
    q = _silu(p[:, :W])
    f = p[:, W:2 * W]
    v = p[:, 2 * W:3 * W]
    g = p[:, 3 * W:]
    ef = jnp.exp(-jnp.abs(f))
    lsig = jnp.minimum(f, 0.0) - jnp.log(1.0 + ef)
    c1 = llb_ref[...]
    c2 = l1lb_ref[...] + lsig
    log_f = jnp.maximum(c1, c2) + jnp.log(1.0 + jnp.exp(-jnp.abs(c1 - c2)))
    kv = omlb_ref[...] * (jnp.where(f >= 0.0, ef, 1.0) / (1.0 + ef))
    log2_f = log_f * LOG2E
    d_all = [_dot3(mseg_ref[...], log2_f[n * C:(n + 1) * C]) for n in range(NB)]
    b = jnp.concatenate([d[:C] for d in d_all], axis=0)
    b_last = [d[C - 1:C, :] for d in d_all]
    q_b = q.astype(BF16)
    kv_b = kv.astype(BF16)
    qe = jnp.exp2(b).astype(BF16) * q_b
    ke = jnp.exp2(jnp.concatenate([jnp.broadcast_to(bl, (C, W)) for bl in b_last], axis=0) - b).astype(BF16) * kv_b
    e_last = [jnp.exp2(bl) for bl in b_last]
    qk = q * kv
    vb = v.astype(BF16)
    rowi = lax.broadcasted_iota(jnp.int32, (NB * C, 1), 0)
    ti = lax.broadcasted_iota(jnp.int32, (C, C), 0)
    tj = lax.broadcasted_iota(jnp.int32, (C, C), 1)
    zs, masks = [], []
    hsz = C // 2
    lvl = 1
    while hsz >= 1:
        is_q = (rowi & hsz) != 0
        seg = jnp.concatenate([d[lvl * C:(lvl + 1) * C] for d in d_all], axis=0)
        zs.append(jnp.exp2(seg).astype(BF16) * jnp.where(is_q, q_b, kv_b))
        masks.append(((ti & hsz) != 0) & ((tj & hsz) == 0) & ((ti // (2 * hsz)) == (tj // (2 * hsz))))
        hsz //= 2
        lvl += 1
    chains = [(n, h) for n in range(NB) for h in range(H)]
    blk = lambda x, c: x[c[0] * C:(c[0] + 1) * C, c[1] * HGRN_HEAD:(c[1] + 1) * HGRN_HEAD]
    att = {}
    for z, mask in zip(zs, masks):
        for c in chains:
            zh = blk(z, c)
            term = jnp.where(mask, _bdot_nt(zh, zh), 0.0)
            att[c] = term if c not in att else att[c] + term
    o = {c: (_bdot(att[c], blk(vb, c)) + _bdot_nt(blk(qe, c), st_ref[c[0], c[1]])
             + jnp.sum(blk(qk, c), axis=-1, keepdims=True) * blk(v, c)) for c in chains}
    for c in chains:
        n, h = c
        st_ref[n, h] = (st_ref[n, h] * e_last[n][:, h * HGRN_HEAD:(h + 1) * HGRN_HEAD]
                        + _bdot_tn(blk(vb, c), blk(ke, c)))
    for c in chains:
        o[c] = o[c] * lax.rsqrt(jnp.mean(o[c] * o[c], axis=-1, keepdims=True) + NORM_EPS) * gn_ref[...]
    o = jnp.concatenate([jnp.concatenate([o[(n, h)] for h in range(H)], axis=1) for n in range(NB)], axis=0)
    out = (o * _silu(g)).astype(y_ref.dtype)
    for n in range(NB):
        y_ref[n] = out[n * C:(n + 1) * C]

    @pl.when(t == pl.num_programs(1) - 1)
    def _():
        for n in range(NB):
            for h in range(H):
                sT_ref[n, h] = st_ref[n, h].T


def _hgrn(p, s0, lb, gnorm):
    B, L, cols = p.shape
    W = cols // 4
    H = W // HGRN_HEAD
    C = _time_tile(L, 64)
    assert 1 << int(math.log2(C)) == C
    mseg = _tile3(_hgrn_segment_matrix(C))
    nb = HGRN_SEQS if B % HGRN_SEQS == 0 else 1
    row = lambda x: x.reshape(1, -1)
    y, sT = pl.pallas_call(
        functools.partial(_hgrn_kernel, width=W),
        out_shape=[jax.ShapeDtypeStruct((B, L, W), BF16),
                   jax.ShapeDtypeStruct((B, H, HGRN_HEAD, HGRN_HEAD), F32)],
        grid=(B // nb, L // C),
        in_specs=[pl.BlockSpec((nb, C, cols), lambda b, t: (b, t, 0)),
                  pl.BlockSpec((nb, H, HGRN_HEAD, HGRN_HEAD), lambda b, t: (b, 0, 0, 0)),
                  _const_spec((1, W)), _const_spec((1, W)), _const_spec((1, W)), _const_spec((1, HGRN_HEAD)),
                  _const_spec(mseg.shape)],
        out_specs=[pl.BlockSpec((nb, C, W), lambda b, t: (b, t, 0)),
                   pl.BlockSpec((nb, H, HGRN_HEAD, HGRN_HEAD), lambda b, t: (b, 0, 0, 0))],
        scratch_shapes=[pltpu.VMEM((nb, H, HGRN_HEAD, HGRN_HEAD), F32)],
        compiler_params=_cparams(("parallel", "arbitrary")),
        name="hgrn2_mixer",
    )(p, s0, row(jnp.log(lb)), row(jnp.log1p(-lb)), row(1.0 - lb), row(gnorm), mseg)
    return y, sT


def _trunk(x, mods_all, s5_re, s5_im, rwkv_s, rwkv_shift, hgrn_s, P, Pb):
    depth = P['norm_mix'].shape[0]
    lb_all = jax.nn.softmax(P['hgrn_lower_bounds'], axis=0)
    lb_all = jnp.cumsum(lb_all, axis=0) - lb_all[0]
    s5_width = P['s5_w_glu'].shape[1]
    n_re, n_im, n_rw, n_sh, n_hg = [], [], [], [], []
    B, L, D = x.shape
    flat = L < FLAT_BELOW
    if flat:
        x = x.reshape(1, B * L, D)
        mods_all = jnp.repeat(mods_all, L, axis=1).transpose(0, 2, 1, 3)[:, None]
    else:
        mods_all = mods_all[:, :, :, None, :]
    seq = lambda a: a.reshape(B, L, a.shape[-1])
    tok = lambda a: a.reshape(x.shape[0], x.shape[1], a.shape[-1])
    for l in range(depth):
        mods = mods_all[l]
        x = _ffn(x, mods, P['norm_ffn1'][l], Pb['ffn1_w1'], Pb['ffn1_w3'], Pb['ffn1_w2'], l, 0)
        i = l // 2
        if l % 2 == 0:
            n_in = Pb['ab_w_in'].shape[2]
            u, pr = _inproj(x, mods, P['norm_mix'][l], Pb['ab_w_in'][i], (s5_width, n_in - s5_width))
            ya, hr, hi = _s5(seq(u), s5_re[i], s5_im[i], P['s5_lam_re'][i], P['s5_lam_im'][i],
                             P['s5_log_dt'][i], P['s5_b_re'][i], P['s5_b_im'][i], P['s5_c_re'][i],
                             P['s5_c_im'][i], P['s5_d'][i], Pb['s5_w_glu'][i])
            yb, sh_new, rw_new = _rwkv(seq(pr), rwkv_shift[i], rwkv_s[i], P['rwkv_mu'][i], P['rwkv_w0'][i],
                                       P['rwkv_w_w2'][i], P['rwkv_a0'][i], P['rwkv_w_a2'][i],
                                       P['rwkv_w_g2'][i], P['rwkv_k_k'][i], P['rwkv_k_a'][i],
                                       P['rwkv_r_k'][i], P['rwkv_lnx_g'][i], P['rwkv_lnx_b'][i])
            ys, w_out = (tok(ya), tok(yb)), Pb['ab_w_out'][i]
            n_re.append(hr)
            n_im.append(hi)
            n_rw.append(rw_new)
            n_sh.append(sh_new)
        else:
            (pc,) = _inproj(x, mods, P['norm_mix'][l], Pb['c_w_in'][i], (Pb['c_w_in'].shape[2],))
            yc, hg_new = _hgrn(seq(pc), hgrn_s[i], lb_all[l], P['hgrn_gnorm'][i])
            ys, w_out = (tok(yc),), Pb['c_w_out'][i]
            n_hg.append(hg_new)
        fin = P['final_norm'] if l == depth - 1 else None
        x = _ffn(x, mods, P['norm_ffn2'][l], Pb['ffn2_w1'], Pb['ffn2_w3'], Pb['ffn2_w2'], l, 2,
                 final_gain=fin, ys=ys, w_out=w_out)
    return (x.reshape(B, L, D), jnp.stack(n_re), jnp.stack(n_im), jnp.stack(n_rw), jnp.stack(n_sh),
            jnp.stack(n_hg))


def kernel(x_prompt, x_sample, state_s5_re, state_s5_im, state_rwkv, state_rwkv_shift, state_hgrn, c_prompt, c_sample, w_ada, b_ada, norm_ffn1, norm_mix, norm_ffn2, ffn1_w1, ffn1_w3, ffn1_w2, ffn2_w1, ffn2_w3, ffn2_w2, ab_w_in, ab_w_out, s5_lam_re, s5_lam_im, s5_log_dt, s5_b_re, s5_b_im, s5_c_re, s5_c_im, s5_d, s5_w_glu, rwkv_mu, rwkv_w0, rwkv_w_w2, rwkv_a0, rwkv_w_a2, rwkv_w_g2, rwkv_k_k, rwkv_k_a, rwkv_r_k, rwkv_lnx_g, rwkv_lnx_b, c_w_in, c_w_out, hgrn_lower_bounds, hgrn_gnorm, final_norm):
    P = dict(norm_ffn1=norm_ffn1, norm_mix=norm_mix, norm_ffn2=norm_ffn2,
             s5_lam_re=s5_lam_re, s5_lam_im=s5_lam_im, s5_log_dt=s5_log_dt, s5_b_re=s5_b_re, s5_b_im=s5_b_im,
             s5_c_re=s5_c_re, s5_c_im=s5_c_im, s5_d=s5_d, s5_w_glu=s5_w_glu, rwkv_mu=rwkv_mu,
             rwkv_w0=rwkv_w0, rwkv_w_w2=rwkv_w_w2, rwkv_a0=rwkv_a0, rwkv_w_a2=rwkv_w_a2, rwkv_w_g2=rwkv_w_g2,
             rwkv_k_k=rwkv_k_k, rwkv_k_a=rwkv_k_a, rwkv_r_k=rwkv_r_k, rwkv_lnx_g=rwkv_lnx_g,
             rwkv_lnx_b=rwkv_lnx_b, hgrn_lower_bounds=hgrn_lower_bounds, hgrn_gnorm=hgrn_gnorm,
             final_norm=final_norm)
    Pb = {n: w.astype(BF16) for n, w in dict(
        ffn1_w1=ffn1_w1, ffn1_w3=ffn1_w3, ffn1_w2=ffn1_w2, ffn2_w1=ffn2_w1, ffn2_w3=ffn2_w3, ffn2_w2=ffn2_w2,
        ab_w_in=ab_w_in, ab_w_out=ab_w_out, c_w_in=c_w_in, c_w_out=c_w_out, s5_w_glu=s5_w_glu).items()}
    depth, d = norm_mix.shape
    bp, bs = x_prompt.shape[0], x_sample.shape[0]
    mods = _ada(jnp.concatenate([c_prompt, c_sample], axis=0), w_ada, b_ada)
    mods = mods.reshape(depth, bp + bs, N_MOD, d)
    dt = x_prompt.dtype
    zeros_like_b = lambda s: jnp.zeros((s.shape[0], bp) + s.shape[2:], dt)
    outs_p = _trunk(x_prompt, mods[:, :bp], zeros_like_b(state_s5_re), zeros_like_b(state_s5_im),
                    zeros_like_b(state_rwkv), zeros_like_b(state_rwkv_shift), zeros_like_b(state_hgrn), P, Pb)
    outs_s = _trunk(x_sample, mods[:, bp:], state_s5_re, state_s5_im, state_rwkv, state_rwkv_shift,
                    state_hgrn, P, Pb)
    return (outs_p[0], outs_s[0]) + tuple(outs_p[1:]) + tuple(outs_s[1:])
```

```python
import functools
import math

import jax
import jax.numpy as jnp
from jax import lax
from jax.experimental import pallas as pl
from jax.experimental.pallas import tpu as pltpu

F32 = jnp.float32
BF16 = jnp.bfloat16

NORM_EPS = 1e-6
LOG2E = 1.4426950408889634
RWKV_GN_EPS = 64e-5
N_MOD = 9
RWKV_HEAD = 64
RWKV_BASE = 8
HGRN_HEAD = 128
LANES = 128
MXU_COLS = 256
S5_SEQS = 8
S5_STEPS = 64
RWKV_SEQS = 4
HGRN_SEQS = 4
VMEM_LIMIT = 56 * 1024 * 1024
FFN_ROWS = 1024
FLAT_BELOW = 64


def _bdot(a, b):
    return jnp.dot(a.astype(BF16), b.astype(BF16), preferred_element_type=F32)


def _bdot_nt(a, b):
    return lax.dot_general(a.astype(BF16), b.astype(BF16), (((1,), (1,)), ((), ())),
                           preferred_element_type=F32)


def _bdot_tn(a, b):
    return lax.dot_general(a.astype(BF16), b.astype(BF16), (((0,), (0,)), ((), ())),
                           preferred_element_type=F32)


def _dot2(a, b2):
    hi = a.astype(BF16)
    lo = (a - hi.astype(F32)).astype(BF16)
    return jnp.dot(jnp.concatenate([hi, lo], axis=1), b2, preferred_element_type=F32)


def _dot3(m3, x):
    hi = x.astype(BF16)
    r1 = x - hi.astype(F32)
    mid = r1.astype(BF16)
    lo = (r1 - mid.astype(F32)).astype(BF16)
    return jnp.dot(m3, jnp.concatenate([hi, mid, lo], axis=0), preferred_element_type=F32)


def _tile3(m):
    return jnp.concatenate([m, m, m], axis=1)


def _sigmoid(x):
    return 1.0 / (1.0 + jnp.exp(-x))


def _silu(x):
    return x * _sigmoid(x)


def _softplus(x):
    return jnp.maximum(x, 0.0) + jnp.log(1.0 + jnp.exp(-jnp.abs(x)))


def _cparams(sem):
    return pltpu.CompilerParams(dimension_semantics=sem, vmem_limit_bytes=VMEM_LIMIT)


def _const_spec(shape):
    nd = len(shape)
    return pl.BlockSpec(shape, lambda *_: (0,) * nd, pipeline_mode=pl.Buffered(1))


def _time_tile(L, want):
    t = min(L, want)
    assert L % t == 0 and t % 8 == 0
    return t


def _ada_kernel(c_ref, w_ref, b_ref, o_ref):
    cs = _silu(c_ref[...])
    o_ref[0] = _bdot(cs, w_ref[0]) + b_ref[0]


def _ada(c_all, w_ada, b_ada):
    depth, d, n = w_ada.shape
    nb = c_all.shape[0]
    tn = 1024
    return pl.pallas_call(
        _ada_kernel,
        out_shape=jax.ShapeDtypeStruct((depth, nb, n), F32),
        grid=(depth, n // tn),
        in_specs=[pl.BlockSpec((nb, d), lambda l, j: (0, 0)),
                  pl.BlockSpec((1, d, tn), lambda l, j: (l, 0, j)),
                  pl.BlockSpec((1, 1, tn), lambda l, j: (l, 0, j))],
        out_specs=pl.BlockSpec((1, nb, tn), lambda l, j: (l, 0, j)),
        compiler_params=_cparams(("arbitrary", "arbitrary")),
        name="ada_mod",
    )(c_all, w_ada, b_ada.reshape(depth, 1, n))


def _norm_mod(x, gain, shift, scale):
    y = x * lax.rsqrt(jnp.mean(x * x, axis=-1, keepdims=True) + NORM_EPS) * gain
    return y * (1.0 + scale) + shift


def _ffn_kernel(x_ref, mod_ref, g_ref, w1_ref, w3_ref, w2_ref, *rest, which, fc, final, n_y):
    y_refs = rest[:n_y]
    rest = rest[n_y:]
    if n_y:
        wo_ref, rest = rest[0], rest[1:]
    if final:
        gf_ref, o_ref = rest
    else:
        (o_ref,) = rest
    x = x_ref[0]
    if n_y:
        off = 0
        mix = None
        for y_ref in y_refs:
            k = y_ref.shape[2]
            part = jnp.dot(y_ref[0], wo_ref[off:off + k, :], preferred_element_type=F32)
            mix = part if mix is None else mix + part
            off += k
        x = x + (1.0 + mod_ref[0, 5]) * mix
    shift = mod_ref[0, 3 * which]
    scale = mod_ref[0, 3 * which + 1]
    gate = mod_ref[0, 3 * which + 2]
    h = _norm_mod(x, g_ref[...], shift, scale).astype(BF16)
    d_ff = w1_ref.shape[1]
    acc = jnp.zeros(x.shape, F32)
    for c in range(d_ff // fc):
        a = jnp.dot(h, w1_ref[:, c * fc:(c + 1) * fc], preferred_element_type=F32)
        b = jnp.dot(h, w3_ref[:, c * fc:(c + 1) * fc], preferred_element_type=F32)
        u = (_silu(a) * b).astype(BF16)
        acc = acc + jnp.dot(u, w2_ref[c * fc:(c + 1) * fc, :], preferred_element_type=F32)
    y = x + 0.5 * (1.0 + gate) * acc
    if final:
        y = y * lax.rsqrt(jnp.mean(y * y, axis=-1, keepdims=True) + NORM_EPS) * gf_ref[...]
    o_ref[0] = y


def _mod_spec(mods, tm):
    _, n_mod, R, D = mods.shape
    if R == 1:
        return pl.BlockSpec((1, n_mod, 1, D), lambda b, i: (b, 0, 0, 0))
    return pl.BlockSpec((1, n_mod, tm, D), lambda b, i: (b, 0, i, 0))


def _layer_spec(shape, layer):
    nd = len(shape) - 1
    return pl.BlockSpec((None,) + tuple(shape[1:]), lambda *_: (layer,) + (0,) * nd, pipeline_mode=pl.Buffered(1))


def _ffn(x, mods, gain, w1, w3, w2, layer, which, final_gain=None, ys=(), w_out=None):
    B, L, D = x.shape
    d_ff = w1.shape[2]
    tm = _time_tile(L, FFN_ROWS)
    fc = MXU_COLS
    assert d_ff % fc == 0
    final = final_gain is not None
    ins = [x, mods, gain.reshape(1, D), w1, w3, w2]
    specs = [pl.BlockSpec((1, tm, D), lambda b, i: (b, i, 0)), _mod_spec(mods, tm), _const_spec((1, D)),
             _layer_spec(w1.shape, layer), _layer_spec(w3.shape, layer), _layer_spec(w2.shape, layer)]
    if ys:
        assert sum(y.shape[2] for y in ys) == w_out.shape[0]
        ins += list(ys) + [w_out]
        specs += [pl.BlockSpec((1, tm, y.shape[2]), lambda b, i: (b, i, 0)) for y in ys]
        specs.append(_const_spec(w_out.shape))
    if final:
        ins.append(final_gain.reshape(1, D))
        specs.append(_const_spec((1, D)))
    return pl.pallas_call(
        functools.partial(_ffn_kernel, which=which, fc=fc, final=final, n_y=len(ys)),
        out_shape=jax.ShapeDtypeStruct((B, L, D), F32),
        grid=(B, L // tm),
        in_specs=specs,
        out_specs=pl.BlockSpec((1, tm, D), lambda b, i: (b, i, 0)),
        compiler_params=_cparams(("parallel", "parallel")),
        name="ffn",
    )(*ins)


def _inproj_kernel(x_ref, mod_ref, g_ref, w_ref, *o_refs, splits):
    x = x_ref[0]
    h = _norm_mod(x, g_ref[...], mod_ref[0, 3], mod_ref[0, 4]).astype(BF16)
    off = 0
    for o_ref, n in zip(o_refs, splits):
        o_ref[0] = jnp.dot(h, w_ref[:, off:off + n], preferred_element_type=F32)
        off += n


def _inproj(x, mods, gain, w_in, splits):
    B, L, D = x.shape
    n = w_in.shape[1]
    assert sum(splits) == n
    tm = _time_tile(L, FFN_ROWS)
    return pl.pallas_call(
        functools.partial(_inproj_kernel, splits=splits),
        out_shape=[jax.ShapeDtypeStruct((B, L, s), F32) for s in splits],
        grid=(B, L // tm),
        in_specs=[pl.BlockSpec((1, tm, D), lambda b, i: (b, i, 0)), _mod_spec(mods, tm),
                  _const_spec((1, D)), _const_spec((D, n))],
        out_specs=[pl.BlockSpec((1, tm, s), lambda b, i: (b, i, 0)) for s in splits],
        compiler_params=_cparams(("parallel", "parallel")),
        name="mixer_in_proj",
    )(x, mods, gain.reshape(1, D), w_in)


def _s5_kernel(u_ref, h0_ref, perm_ref, permt_ref, wb_ref, wc_ref, a1r_ref, a1i_ref, d_ref, wg_ref,
               y_ref, hT_ref, carry_ref, *, tseg):
    t = pl.program_id(1)

    @pl.when(t == 0)
    def _():
        carry_ref[...] = h0_ref[...]

    u = jnp.concatenate([u_ref[k] for k in range(S5_SEQS)], axis=0)
    u = _dot3(perm_ref[...], u)
    nblk = u.shape[1] // LANES
    half = wb_ref.shape[2] // 2
    ys = []
    for j in range(nblk):
        cols = slice(j * half, (j + 1) * half)
        re_cols = slice(2 * j * half, (2 * j + 1) * half)
        im_cols = slice((2 * j + 1) * half, (2 * j + 2) * half)
        bu = _bdot(u[:, j * LANES:(j + 1) * LANES], wb_ref[j])
        ar = a1r_ref[:, cols]
        ai = a1i_ref[:, cols]
        sr = carry_ref[:, re_cols]
        si = carry_ref[:, im_cols]
        xr_rows, xi_rows = [], []
        for i in range(tseg):
            br = bu[S5_SEQS * i:S5_SEQS * (i + 1), :half]
            bi = bu[S5_SEQS * i:S5_SEQS * (i + 1), half:]
            sr, si = ar * sr - ai * si + br, ar * si + ai * sr + bi
            xr_rows.append(sr)
            xi_rows.append(si)
        carry_ref[:, re_cols] = sr
        carry_ref[:, im_cols] = si
        xr = jnp.concatenate(xr_rows, axis=0)
        xi = jnp.concatenate(xi_rows, axis=0)
        ys.append(_bdot(xr, wc_ref[j, :half, :]) + _bdot(xi, wc_ref[j, half:, :]))
    y = jnp.concatenate(ys, axis=1) + u * d_ref[...]
    y = jax.nn.gelu(y)
    y = (y * _sigmoid(_bdot(y, wg_ref[...]))).astype(BF16)
    y = jnp.dot(permt_ref[...], y, preferred_element_type=F32).astype(y_ref.dtype)
    for k in range(S5_SEQS):
        y_ref[k] = y[k * tseg:(k + 1) * tseg]
    hT_ref[...] = carry_ref[...]


def _s5_tables(lam_re, lam_im, log_dt, b_re, b_im, c_re, c_im):
    G, P = lam_re.shape
    H = b_re.shape[2]
    gpb = LANES // H
    nblk = G // gpb
    dt = jnp.exp(log_dt)[:, None]
    mag = jnp.exp(lam_re * dt)
    ab_re = mag * jnp.cos(lam_im * dt)
    ab_im = mag * jnp.sin(lam_im * dt)
    den = lam_re * lam_re + lam_im * lam_im
    z_re = ((ab_re - 1.0) * lam_re + ab_im * lam_im) / den
    z_im = (ab_im * lam_re - (ab_re - 1.0) * lam_im) / den
    bb_re = z_re[..., None] * b_re - z_im[..., None] * b_im
    bb_im = z_re[..., None] * b_im + z_im[..., None] * b_re
    eye = jnp.eye(gpb, dtype=F32)

    def blockdiag_in(bb):
        bb = bb.reshape(nblk, gpb, P, H)
        m = jnp.einsum('jgph,gk->jghkp', bb, eye)
        return m.reshape(nblk, gpb * H, gpb * P)

    def blockdiag_out(cc):
        cc = cc.reshape(nblk, gpb, H, P)
        m = jnp.einsum('jghp,gk->jgpkh', cc, eye)
        return m.reshape(nblk, gpb * P, gpb * H)

    wb = jnp.concatenate([blockdiag_in(bb_re), blockdiag_in(bb_im)], axis=2).astype(BF16)
    wc = jnp.concatenate([blockdiag_out(c_re), blockdiag_out(-c_im)], axis=1).astype(BF16)

    a1r = jnp.broadcast_to(ab_re.reshape(1, G * P), (S5_SEQS, G * P))
    a1i = jnp.broadcast_to(ab_im.reshape(1, G * P), (S5_SEQS, G * P))
    return wb, wc, a1r, a1i


def _s5_state_to_cols(h_re, h_im, nblk):
    B, G, P = h_re.shape
    gpb = G // nblk
    st = jnp.stack([h_re.reshape(B, nblk, gpb * P), h_im.reshape(B, nblk, gpb * P)], axis=2)
    return st.reshape(B, 2 * G * P)


def _s5_cols_to_state(cols, G, P, nblk):
    B = cols.shape[0]
    st = cols.reshape(B, nblk, 2, (G // nblk) * P)
    return st[:, :, 0].reshape(B, G, P), st[:, :, 1].reshape(B, G, P)


def _s5(u, h_re, h_im, lam_re, lam_im, log_dt, b_re, b_im, c_re, c_im, d_skip, w_glu_bf16):
    B, L, W = u.shape
    G, P = lam_re.shape
    assert B % S5_SEQS == 0
    tseg = _time_tile(L, S5_STEPS)
    T = S5_SEQS * tseg
    nblk = W // LANES
    tables = _s5_tables(lam_re, lam_im, log_dt, b_re, b_im, c_re, c_im)
    r = jnp.arange(T)
    perm = ((r % S5_SEQS) * tseg + r // S5_SEQS)[:, None] == jnp.arange(T)[None, :]
    perm = perm.astype(BF16)
    h0 = _s5_state_to_cols(h_re, h_im, nblk)
    ncol = 2 * G * P
    consts = (_tile3(perm), perm.T) + tables + (d_skip.reshape(1, W), w_glu_bf16)
    y, hT = pl.pallas_call(
        functools.partial(_s5_kernel, tseg=tseg),
        out_shape=[jax.ShapeDtypeStruct((B, L, W), BF16), jax.ShapeDtypeStruct((B, ncol), F32)],
        grid=(B // S5_SEQS, L // tseg),
        in_specs=[pl.BlockSpec((S5_SEQS, tseg, W), lambda b, t: (b, t, 0)),
                  pl.BlockSpec((S5_SEQS, ncol), lambda b, t: (b, 0))]
        + [_const_spec(c.shape) for c in consts],
        out_specs=[pl.BlockSpec((S5_SEQS, tseg, W), lambda b, t: (b, t, 0)),
                   pl.BlockSpec((S5_SEQS, ncol), lambda b, t: (b, 0))],
        scratch_shapes=[pltpu.VMEM((S5_SEQS, ncol), F32)],
        compiler_params=_cparams(("parallel", "arbitrary")),
        name="s5_mixer",
    )(u, h0, *consts)
    hr, hi = _s5_cols_to_state(hT, G, P, nblk)
    return y, hr, hi


def _stack_heads(x, lane_lo):
    zero = jnp.zeros_like(x)
    return jnp.concatenate([jnp.where(lane_lo, x, zero), jnp.where(lane_lo, zero, x)], axis=0)


def _head_sum(x, hm2):
    nblk = x.shape[1] // LANES
    R = x.shape[0]
    xs = jnp.concatenate([x[:, j * LANES:(j + 1) * LANES] for j in range(nblk)], axis=0)
    s = _dot2(xs, hm2)
    return jnp.concatenate([s[j * R:(j + 1) * R] for j in range(nblk)], axis=1)


def _rwkv_kernel(p_ref, sp_ref, s0_ref, mu_ref, w0_ref, ww2_ref, a0_ref, wa2_ref, wg2_ref, kk_ref, ka_ref,
                 rk_ref, lng_ref, lnb_ref, hm_ref, tri_ref, y_ref, sh_ref, sT_ref, prev_ref, st_ref, *, width):
    t = pl.program_id(1)
    W = width
    NB, C, _ = p_ref.shape

    @pl.when(t == 0)
    def _():
        prev_ref[...] = sp_ref[...]
        st_ref[...] = s0_ref[...]

    rows = lax.broadcasted_iota(jnp.int32, (C, 1), 0)
    ps = []
    for n in range(NB):
        p = p_ref[n]
        p_prev = jnp.where(rows == 0, prev_ref[n], pltpu.roll(p, 1, axis=0))
        prev_ref[n] = p[C - 1:C, :]
        sh_ref[n] = p[C - 1:C, :]
        ps.append(p + (p_prev - p) * mu_ref[...])
    ps = jnp.concatenate(ps, axis=0)
    r = ps[:, :W]
    k = ps[:, W:2 * W]
    v = ps[:, 2 * W:3 * W]
    wa_low = ps[:, 3 * W:3 * W + LANES]
    g_low = ps[:, 3 * W + LANES:]
    xw = w0_ref[...] + _bdot(jnp.tanh(wa_low), ww2_ref[...])
    lw = -LOG2E * jnp.exp(-_softplus(-xw) - 0.5)
    a = _sigmoid(a0_ref[...] + _bdot(wa_low, wa2_ref[...]))
    g = _bdot(_sigmoid(g_low), wg2_ref[...])
    hm = hm_ref[...]
    kk = k * kk_ref[...]
    kk = kk / jnp.maximum(jnp.sqrt(_head_sum(kk * kk, hm)), 1e-12)
    kmod = k * (1.0 + (a - 1.0) * ka_ref[...])
    bvec = kk * a

    cum = _dot3(tri_ref[...], lw)
    e_in = jnp.exp2(cum)
    e_ex = jnp.exp2(cum - lw)
    e_neg = jnp.exp2(-cum)
    cum_last = [cum[(n + 1) * C - 1:(n + 1) * C, :] for n in range(NB)]
    e_end = jnp.exp2(jnp.concatenate([jnp.broadcast_to(c, (C, W)) for c in cum_last], axis=0) - cum)
    w_end = [jnp.exp2(c) for c in cum_last]
    at = (-kk * e_ex).astype(BF16)
    rt = (r * e_in).astype(BF16)
    bt = (bvec * e_neg).astype(BF16)
    kt = (kmod * e_neg).astype(BF16)
    bw = (bvec * e_end).astype(BF16)
    kw = (kmod * e_end).astype(BF16)
    vb = v.astype(BF16)

    C2 = 2 * C
    si = lax.broadcasted_iota(jnp.int32, (C, C2), 0)
    sj = lax.broadcasted_iota(jnp.int32, (C, C2), 1) % C
    strict = si > sj
    lower = si >= sj
    lane_lo = lax.broadcasted_iota(jnp.int32, (1, LANES), 1) < RWKV_HEAD
    side_lo = lax.broadcasted_iota(jnp.int32, (1, C2), 1) < C
    npair = W // LANES
    chains = [(n, j) for n in range(NB) for j in range(npair)]
    x0, vst, bkw, l_ab, a_ak, a_r = {}, {}, {}, {}, {}, {}
    for c in chains:
        n, j = c
        pair = lambda x: x[n * C:(n + 1) * C, j * LANES:(j + 1) * LANES]
        ar = jnp.concatenate([pair(at), pair(rt)], axis=0)
        bk_st = jnp.concatenate([_stack_heads(pair(bt), lane_lo), _stack_heads(pair(kt), lane_lo)], axis=0)
        m = _bdot_nt(ar, bk_st)
        x0[c] = _bdot_nt(ar, st_ref[n, j])
        vst[c] = _stack_heads(pair(vb), lane_lo)
        bkw[c] = jnp.concatenate([_stack_heads(pair(bw), lane_lo), _stack_heads(pair(kw), lane_lo)], axis=0)
        l_ab[c] = jnp.where(strict, m[:C, :C2], 0.0)
        a_ak[c] = jnp.where(strict, m[:C, C2:], 0.0).astype(BF16)
        a_r[c] = jnp.concatenate([jnp.where(lower, m[C:, :C2], 0.0), jnp.where(lower, m[C:, C2:], 0.0)],
                                 axis=1).astype(BF16)
    def split(x):
        hi = x.astype(BF16)
        return hi, (x - hi.astype(F32)).astype(BF16)

    def lhs3(x):
        hi, lo = split(x)
        return jnp.concatenate([hi, lo, hi], axis=1)

    def rhs3(x, mask):
        hi, lo = split(x)
        hi = _stack_heads(hi, mask)
        return jnp.concatenate([hi, hi, _stack_heads(lo, mask)], axis=0)

    blk_t = si // RWKV_BASE
    blk_s = sj // RWKV_BASE
    diag8 = blk_t == blk_s
    l_q = {c: l_ab[c].astype(BF16) for c in chains}
    zero = jnp.zeros((C, C2), BF16)
    tinv = {c: jnp.where(si == sj, 1.0, 0.0) + jnp.where(diag8, l_q[c], zero).astype(F32) for c in chains}
    pw = {c: jnp.where(diag8, l_q[c], zero) for c in chains}
    for _ in range(int(math.log2(RWKV_BASE)) - 1):
        for c in chains:
            pw[c] = _bdot(pw[c], _stack_heads(pw[c].astype(BF16), side_lo))
        for c in chains:
            tinv[c] = tinv[c] + _bdot(pw[c], _stack_heads(tinv[c].astype(BF16), side_lo))

    def rhs2(x, mask):
        hi, lo = split(x)
        return jnp.concatenate([_stack_heads(hi, mask), _stack_heads(lo, mask)], axis=0)

    b = RWKV_BASE
    while b < C:
        lower_left = ((si // (2 * b)) == (sj // (2 * b))) & (((si // b) % 2) == 1) & (((sj // b) % 2) == 0)
        l_b = {c: jnp.where(lower_left, l_q[c], zero) for c in chains}
        lt = {c: jnp.dot(jnp.concatenate([l_b[c], l_b[c]], axis=1), rhs2(tinv[c], side_lo),
                         preferred_element_type=F32) for c in chains}
        for c in chains:
            tinv[c] = tinv[c] + jnp.dot(lhs3(tinv[c]), rhs3(lt[c], side_lo), preferred_element_type=F32)
        b *= 2
    z = {c: x0[c][:C] + jnp.dot(a_ak[c], vst[c], preferred_element_type=F32) for c in chains}
    u = {c: jnp.dot(lhs3(tinv[c]), rhs3(z[c], lane_lo), preferred_element_type=F32) for c in chains}
    uv = {c: jnp.concatenate([_stack_heads(u[c].astype(BF16), lane_lo), vst[c]], axis=0) for c in chains}
    y_p = {c: x0[c][C:] + jnp.dot(a_r[c], uv[c], preferred_element_type=F32) for c in chains}
    for c in chains:
        n, j = c
        st_ref[n, j] = st_ref[n, j] * w_end[n][:, j * LANES:(j + 1) * LANES] + _bdot_tn(uv[c], bkw[c])
    y = jnp.concatenate([jnp.concatenate([y_p[(n, j)] for j in range(npair)], axis=1)
                         for n in range(NB)], axis=0)
    inv_n = 1.0 / RWKV_HEAD
    mean = _head_sum(y, hm) * inv_n
    yc = y - mean
    var = _head_sum(yc * yc, hm) * inv_n
    yn = yc * lax.rsqrt(var + RWKV_GN_EPS) * lng_ref[...] + lnb_ref[...]
    bonus = _head_sum(r * kmod * rk_ref[...], hm) * v
    out = ((yn + bonus) * g).astype(y_ref.dtype)
    for n in range(NB):
        y_ref[n] = out[n * C:(n + 1) * C]

    @pl.when(t == pl.num_programs(1) - 1)
    def _():
        sT_ref[...] = st_ref[...]


def _rwkv(p, shift_prev, s0, mu, w0, w_w2, a0, w_a2, w_g2, k_k, k_a, r_k, lnx_g, lnx_b):
    B, L, cols = p.shape
    H, N = s0.shape[1], s0.shape[2]
    W = H * N
    C = _time_tile(L, 64)
    assert 1 << int(math.log2(C)) == C and 2 * N == LANES
    npair = H // 2
    s0p = s0.reshape(B, npair, 2, N, N)
    z = jnp.zeros_like(s0p[:, :, 0])
    s0bd = jnp.concatenate([jnp.concatenate([s0p[:, :, 0], z], axis=-1),
                            jnp.concatenate([z, s0p[:, :, 1]], axis=-1)], axis=-2)
    lo = w_w2.shape[0]
    ww2p = jnp.concatenate([w_w2, jnp.zeros((LANES - lo, W), F32)], axis=0).astype(BF16)
    wa2p = jnp.concatenate([jnp.zeros((lo, W), F32), w_a2], axis=0).astype(BF16)
    hid = jnp.arange(LANES) // N
    hm = (hid[:, None] == hid[None, :]).astype(BF16)
    hm2 = jnp.concatenate([hm, hm], axis=0)
    nb = RWKV_SEQS if B % RWKV_SEQS == 0 else 1
    tr = jnp.arange(nb * C)
    tri = ((tr[:, None] >= tr[None, :]) & (tr[:, None] // C == tr[None, :] // C)).astype(BF16)
    row = lambda x: x.reshape(1, -1)
    y, sh, sT = pl.pallas_call(
        functools.partial(_rwkv_kernel, width=W),
        out_shape=[jax.ShapeDtypeStruct((B, L, W), BF16),
                   jax.ShapeDtypeStruct((B, 1, cols), F32),
                   jax.ShapeDtypeStruct((B, npair, LANES, LANES), F32)],
        grid=(B // nb, L // C),
        in_specs=[pl.BlockSpec((nb, C, cols), lambda b, t: (b, t, 0)),
                  pl.BlockSpec((nb, 1, cols), lambda b, t: (b, 0, 0)),
                  pl.BlockSpec((nb, npair, LANES, LANES), lambda b, t: (b, 0, 0, 0)),
                  _const_spec((1, cols)), _const_spec((1, W)), _const_spec((LANES, W)), _const_spec((1, W)),
                  _const_spec((LANES, W)), _const_spec((LANES, W)), _const_spec((1, W)), _const_spec((1, W)),
                  _const_spec((1, W)), _const_spec((1, W)), _const_spec((1, W)), _const_spec((2 * LANES, LANES)),
                  _const_spec((nb * C, 3 * nb * C))],
        out_specs=[pl.BlockSpec((nb, C, W), lambda b, t: (b, t, 0)),
                   pl.BlockSpec((nb, 1, cols), lambda b, t: (b, 0, 0)),
                   pl.BlockSpec((nb, npair, LANES, LANES), lambda b, t: (b, 0, 0, 0))],
        scratch_shapes=[pltpu.VMEM((nb, 1, cols), F32), pltpu.VMEM((nb, npair, LANES, LANES), F32)],
        compiler_params=_cparams(("parallel", "arbitrary")),
        name="rwkv7_mixer",
    )(p, shift_prev.reshape(B, 1, cols), s0bd, row(mu), row(w0), ww2p, row(a0), wa2p, w_g2.astype(BF16),
      row(k_k), row(k_a), row(r_k), row(lnx_g), row(lnx_b), hm2, _tile3(tri))
    sT = sT.reshape(B, npair, 2, N, 2, N)
    s_new = jnp.stack([sT[:, :, 0, :, 0, :], sT[:, :, 1, :, 1, :]], axis=2).reshape(B, H, N, N)
    return y, sh[:, 0], s_new


def _hgrn_segment_matrix(C):
    t = jnp.arange(C)[:, None]
    r = jnp.arange(C)[None, :]
    blocks = [r <= t]
    h = C // 2
    while h >= 1:
        mid = (t // (2 * h)) * (2 * h) + h - 1
        is_q = (t & h) != 0
        blocks.append(jnp.where(is_q, (r > mid) & (r <= t), (r > t) & (r <= mid)))
        h //= 2
    return jnp.concatenate(blocks, axis=0).astype(BF16)


def _hgrn_kernel(p_ref, s0_ref, llb_ref, l1lb_ref, omlb_ref, gn_ref, mseg_ref, y_ref, sT_ref, st_ref, *, width):
    t = pl.program_id(1)
    W = width
    H = W // HGRN_HEAD
    NB, C, _ = p_ref.shape

    @pl.when(t == 0)
    def _():
        for n in range(NB):
            for h in range(H):
                st_ref[n, h] = s0_ref[n, h].T

    p = jnp.concatenate([p_ref[n] for n in range(NB)], axis=0)
    q = _silu(p[:, :W])
    f = p[:, W:2 * W]
    v = p[:, 2 * W:3 * W]
    g = p[:, 3 * W:]
    ef = jnp.exp(-jnp.abs(f))
    lsig = jnp.minimum(f, 0.0) - jnp.log(1.0 + ef)
    c1 = llb_ref[...]
    c2 = l1lb_ref[...] + lsig
    log_f = jnp.maximum(c1, c2) + jnp.log(1.0 + jnp.exp(-jnp.abs(c1 - c2)))
    kv = omlb_ref[...] * (jnp.where(f >= 0.0, ef, 1.0) / (1.0 + ef))
    log2_f = log_f * LOG2E
    d_all = [_dot3(mseg_ref[...], log2_f[n * C:(n + 1) * C]) for n in range(NB)]
    b = jnp.concatenate([d[:C] for d in d_all], axis=0)
    b_last = [d[C - 1:C, :] for d in d_all]
    q_b = q.astype(BF16)
    kv_b = kv.astype(BF16)
    qe = jnp.exp2(b).astype(BF16) * q_b
    ke = jnp.exp2(jnp.concatenate([jnp.broadcast_to(bl, (C, W)) for bl in b_last], axis=0) - b).astype(BF16) * kv_b
    e_last = [jnp.exp2(bl) for bl in b_last]
    qk = q * kv
    vb = v.astype(BF16)
    rowi = lax.broadcasted_iota(jnp.int32, (NB * C, 1), 0)
    ti = lax.broadcasted_iota(jnp.int32, (C, C), 0)
    tj = lax.broadcasted_iota(jnp.int32, (C, C), 1)
    zs, masks = [], []
    hsz = C // 2
    lvl = 1
    while hsz >= 1:
        is_q = (rowi & hsz) != 0
        seg = jnp.concatenate([d[lvl * C:(lvl + 1) * C] for d in d_all], axis=0)
        zs.append(jnp.exp2(seg).astype(BF16) * jnp.where(is_q, q_b, kv_b))
        masks.append(((ti & hsz) != 0) & ((tj & hsz) == 0) & ((ti // (2 * hsz)) == (tj // (2 * hsz))))
        hsz //= 2
        lvl += 1
    chains = [(n, h) for n in range(NB) for h in range(H)]
    blk = lambda x, c: x[c[0] * C:(c[0] + 1) * C, c[1] * HGRN_HEAD:(c[1] + 1) * HGRN_HEAD]
    att = {}
    for z, mask in zip(zs, masks):
        for c in chains:
            zh = blk(z, c)
            term = jnp.where(mask, _bdot_nt(zh, zh), 0.0)
            att[c] = term if c not in att else att[c] + term
    o = {c: (_bdot(att[c], blk(vb, c)) + _bdot_nt(blk(qe, c), st_ref[c[0], c[1]])
             + jnp.sum(blk(qk, c), axis=-1, keepdims=True) * blk(v, c)) for c in chains}
    for c in chains:
        n, h = c
        st_ref[n, h] = (st_ref[n, h] * e_last[n][:, h * HGRN_HEAD:(h + 1) * HGRN_HEAD]
                        + _bdot_tn(blk(vb, c), blk(ke, c)))
    for c in chains:
        o[c] = o[c] * lax.rsqrt(jnp.mean(o[c] * o[c], axis=-1, keepdims=True) + NORM_EPS) * gn_ref[...]
    o = jnp.concatenate([jnp.concatenate([o[(n, h)] for h in range(H)], axis=1) for n in range(NB)], axis=0)
    out = (o * _silu(g)).astype(y_ref.dtype)
    for n in range(NB):
        y_ref[n] = out[n * C:(n + 1) * C]

    @pl.when(t == pl.num_programs(1) - 1)
    def _():
        for n in range(NB):
            for h in range(H):
                sT_ref[n, h] = st_ref[n, h].T


def _hgrn(p, s0, lb, gnorm):
    B, L, cols = p.shape
    W = cols // 4
    H = W // HGRN_HEAD
    C = _time_tile(L, 64)
    assert 1 << int(math.log2(C)) == C
    mseg = _tile3(_hgrn_segment_matrix(C))
    nb = HGRN_SEQS if B % HGRN_SEQS == 0 else 1
    row = lambda x: x.reshape(1, -1)
    y, sT = pl.pallas_call(
        functools.partial(_hgrn_kernel, width=W),
        out_shape=[jax.ShapeDtypeStruct((B, L, W), BF16),
                   jax.ShapeDtypeStruct((B, H, HGRN_HEAD, HGRN_HEAD), F32)],
        grid=(B // nb, L // C),
        in_specs=[pl.BlockSpec((nb, C, cols), lambda b, t: (b, t, 0)),
                  pl.BlockSpec((nb, H, HGRN_HEAD, HGRN_HEAD), lambda b, t: (b, 0, 0, 0)),
                  _const_spec((1, W)), _const_spec((1, W)), _const_spec((1, W)), _const_spec((1, HGRN_HEAD)),
                  _const_spec(mseg.shape)],
        out_specs=[pl.BlockSpec((nb, C, W), lambda b, t: (b, t, 0)),
                   pl.BlockSpec((nb, H, HGRN_HEAD, HGRN_HEAD), lambda b, t: (b, 0, 0, 0))],
        scratch_shapes=[pltpu.VMEM((nb, H, HGRN_HEAD, HGRN_HEAD), F32)],
        compiler_params=_cparams(("parallel", "arbitrary")),
        name="hgrn2_mixer",
    )(p, s0, row(jnp.log(lb)), row(jnp.log1p(-lb)), row(1.0 - lb), row(gnorm), mseg)
    return y, sT


def _trunk(x, mods_all, s5_re, s5_im, rwkv_s, rwkv_shift, hgrn_s, P, Pb):
    depth = P['norm_mix'].shape[0]
    lb_all = jax.nn.softmax(P['hgrn_lower_bounds'], axis=0)
    lb_all = jnp.cumsum(lb_all, axis=0) - lb_all[0]
    s5_width = P['s5_w_glu'].shape[1]
    n_re, n_im, n_rw, n_sh, n_hg = [], [], [], [], []
    B, L, D = x.shape
    flat = L < FLAT_BELOW
    if flat:
        x = x.reshape(1, B * L, D)
        mods_all = jnp.repeat(mods_all, L, axis=1).transpose(0, 2, 1, 3)[:, None]
    else:
        mods_all = mods_all[:, :, :, None, :]
    seq = lambda a: a.reshape(B, L, a.shape[-1])
    tok = lambda a: a.reshape(x.shape[0], x.shape[1], a.shape[-1])
    for l in range(depth):
        mods = mods_all[l]
        x = _ffn(x, mods, P['norm_ffn1'][l], Pb['ffn1_w1'], Pb['ffn1_w3'], Pb['ffn1_w2'], l, 0)
        i = l // 2
        if l % 2 == 0:
            n_in = Pb['ab_w_in'].shape[2]
            u, pr = _inproj(x, mods, P['norm_mix'][l], Pb['ab_w_in'][i], (s5_width, n_in - s5_width))
            ya, hr, hi = _s5(seq(u), s5_re[i], s5_im[i], P['s5_lam_re'][i], P['s5_lam_im'][i],
                             P['s5_log_dt'][i], P['s5_b_re'][i], P['s5_b_im'][i], P['s5_c_re'][i],
                             P['s5_c_im'][i], P['s5_d'][i], Pb['s5_w_glu'][i])
            yb, sh_new, rw_new = _rwkv(seq(pr), rwkv_shift[i], rwkv_s[i], P['rwkv_mu'][i], P['rwkv_w0'][i],
                                       P['rwkv_w_w2'][i], P['rwkv_a0'][i], P['rwkv_w_a2'][i],
                                       P['rwkv_w_g2'][i], P['rwkv_k_k'][i], P['rwkv_k_a'][i],
                                       P['rwkv_r_k'][i], P['rwkv_lnx_g'][i], P['rwkv_lnx_b'][i])
            ys, w_out = (tok(ya), tok(yb)), Pb['ab_w_out'][i]
            n_re.append(hr)
            n_im.append(hi)
            n_rw.append(rw_new)
            n_sh.append(sh_new)
        else:
            (pc,) = _inproj(x, mods, P['norm_mix'][l], Pb['c_w_in'][i], (Pb['c_w_in'].shape[2],))
            yc, hg_new = _hgrn(seq(pc), hgrn_s[i], lb_all[l], P['hgrn_gnorm'][i])
            ys, w_out = (tok(yc),), Pb['c_w_out'][i]
            n_hg.append(hg_new)
        fin = P['final_norm'] if l == depth - 1 else None
        x = _ffn(x, mods, P['norm_ffn2'][l], Pb['ffn2_w1'], Pb['ffn2_w3'], Pb['ffn2_w2'], l, 2,
                 final_gain=fin, ys=ys, w_out=w_out)
    return (x.reshape(B, L, D), jnp.stack(n_re), jnp.stack(n_im), jnp.stack(n_rw), jnp.stack(n_sh),
            jnp.stack(n_hg))


def kernel(x_prompt, x_sample, state_s5_re, state_s5_im, state_rwkv, state_rwkv_shift, state_hgrn, c_prompt, c_sample, w_ada, b_ada, norm_ffn1, norm_mix, norm_ffn2, ffn1_w1, ffn1_w3, ffn1_w2, ffn2_w1, ffn2_w3, ffn2_w2, ab_w_in, ab_w_out, s5_lam_re, s5_lam_im, s5_log_dt, s5_b_re, s5_b_im, s5_c_re, s5_c_im, s5_d, s5_w_glu, rwkv_mu, rwkv_w0, rwkv_w_w2, rwkv_a0, rwkv_w_a2, rwkv_w_g2, rwkv_k_k, rwkv_k_a, rwkv_r_k, rwkv_lnx_g, rwkv_lnx_b, c_w_in, c_w_out, hgrn_lower_bounds, hgrn_gnorm, final_norm):
    P = dict(norm_ffn1=norm_ffn1, norm_mix=norm_mix, norm_ffn2=norm_ffn2,
             s5_lam_re=s5_lam_re, s5_lam_im=s5_lam_im, s5_log_dt=s5_log_dt, s5_b_re=s5_b_re, s5_b_im=s5_b_im,
             s5_c_re=s5_c_re, s5_c_im=s5_c_im, s5_d=s5_d, s5_w_glu=s5_w_glu, rwkv_mu=rwkv_mu,
             rwkv_w0=rwkv_w0, rwkv_w_w2=rwkv_w_w2, rwkv_a0=rwkv_a0, rwkv_w_a2=rwkv_w_a2, rwkv_w_g2=rwkv_w_g2,
             rwkv_k_k=rwkv_k_k, rwkv_k_a=rwkv_k_a, rwkv_r_k=rwkv_r_k, rwkv_lnx_g=rwkv_lnx_g,
             rwkv_lnx_b=rwkv_lnx_b, hgrn_lower_bounds=hgrn_lower_bounds, hgrn_gnorm=hgrn_gnorm,
             final_norm=final_norm)
    Pb = {n: w.astype(BF16) for n, w in dict(
        ffn1_w1=ffn1_w1, ffn1_w3=ffn1_w3, ffn1_w2=ffn1_w2, ffn2_w1=ffn2_w1, ffn2_w3=ffn2_w3, ffn2_w2=ffn2_w2,
        ab_w_in=ab_w_in, ab_w_out=ab_w_out, c_w_in=c_w_in, c_w_out=c_w_out, s5_w_glu=s5_w_glu).items()}
    depth, d = norm_mix.shape
    bp, bs = x_prompt.shape[0], x_sample.shape[0]
    mods = _ada(jnp.concatenate([c_prompt, c_sample], axis=0), w_ada, b_ada)
    mods = mods.reshape(depth, bp + bs, N_MOD, d)
    dt = x_prompt.dtype
    zeros_like_b = lambda s: jnp.zeros((s.shape[0], bp) + s.shape[2:], dt)
    outs_p = _trunk(x_prompt, mods[:, :bp], zeros_like_b(state_s5_re), zeros_like_b(state_s5_im),
                    zeros_like_b(state_rwkv), zeros_like_b(state_rwkv_shift), zeros_like_b(state_hgrn), P, Pb)
    outs_s = _trunk(x_sample, mods[:, bp:], state_s5_re, state_s5_im, state_rwkv, state_rwkv_shift,
                    state_hgrn, P, Pb)
    return (outs_p[0], outs_s[0]) + tuple(outs_p[1:]) + tuple(outs_s[1:])
```

```python
import functools
import math

import jax
import jax.numpy as jnp
from jax import lax
from jax.experimental import pallas as pl
from jax.experimental.pallas import tpu as pltpu

F32 = jnp.float32
BF16 = jnp.bfloat16

NORM_EPS = 1e-6
LOG2E = 1.4426950408889634
RWKV_GN_EPS = 64e-5
N_MOD = 9
RWKV_HEAD = 64
RWKV_BASE = 8
HGRN_HEAD = 128
LANES = 128
MXU_COLS = 256
S5_SEQS = 8
S5_STEPS = 64
RWKV_SEQS = 4
HGRN_SEQS = 4
VMEM_LIMIT = 56 * 1024 * 1024
FFN_ROWS = 1024
FLAT_BELOW = 64


def _bdot(a, b):
    return jnp.dot(a.astype(BF16), b.astype(BF16), preferred_element_type=F32)


def _bdot_nt(a, b):
    return lax.dot_general(a.astype(BF16), b.astype(BF16), (((1,), (1,)), ((), ())),
                           preferred_element_type=F32)


def _bdot_tn(a, b):
    return lax.dot_general(a.astype(BF16), b.astype(BF16), (((0,), (0,)), ((), ())),
                           preferred_element_type=F32)


def _dot2(a, b2):
    hi = a.astype(BF16)
    lo = (a - hi.astype(F32)).astype(BF16)
    return jnp.dot(jnp.concatenate([hi, lo], axis=1), b2, preferred_element_type=F32)


def _dot3(m3, x):
    hi = x.astype(BF16)
    r1 = x - hi.astype(F32)
    mid = r1.astype(BF16)
    lo = (r1 - mid.astype(F32)).astype(BF16)
    return jnp.dot(m3, jnp.concatenate([hi, mid, lo], axis=0), preferred_element_type=F32)


def _tile3(m):
    return jnp.concatenate([m, m, m], axis=1)


def _sigmoid(x):
    return 1.0 / (1.0 + jnp.exp(-x))


def _silu(x):
    return x * _sigmoid(x)


def _softplus(x):
    return jnp.maximum(x, 0.0) + jnp.log(1.0 + jnp.exp(-jnp.abs(x)))


def _cparams(sem):
    return pltpu.CompilerParams(dimension_semantics=sem, vmem_limit_bytes=VMEM_LIMIT)


def _const_spec(shape):
    nd = len(shape)
    return pl.BlockSpec(shape, lambda *_: (0,) * nd, pipeline_mode=pl.Buffered(1))


def _time_tile(L, want):
    t = min(L, want)
    assert L % t == 0 and t % 8 == 0
    return t


def _ada_kernel(c_ref, w_ref, b_ref, o_ref):
    cs = _silu(c_ref[...])
    o_ref[0] = _bdot(cs, w_ref[0]) + b_ref[0]


def _ada(c_all, w_ada, b_ada):
    depth, d, n = w_ada.shape
    nb = c_all.shape[0]
    tn = 1024
    return pl.pallas_call(
        _ada_kernel,
        out_shape=jax.ShapeDtypeStruct((depth, nb, n), F32),
        grid=(depth, n // tn),
        in_specs=[pl.BlockSpec((nb, d), lambda l, j: (0, 0)),
                  pl.BlockSpec((1, d, tn), lambda l, j: (l, 0, j)),
                  pl.BlockSpec((1, 1, tn), lambda l, j: (l, 0, j))],
        out_specs=pl.BlockSpec((1, nb, tn), lambda l, j: (l, 0, j)),
        compiler_params=_cparams(("arbitrary", "arbitrary")),
        name="ada_mod",
    )(c_all, w_ada, b_ada.reshape(depth, 1, n))


def _norm_mod(x, gain, shift, scale):
    y = x * lax.rsqrt(jnp.mean(x * x, axis=-1, keepdims=True) + NORM_EPS) * gain
    return y * (1.0 + scale) + shift


def _ffn_kernel(x_ref, mod_ref, g_ref, w1_ref, w3_ref, w2_ref, *rest, which, fc, final, n_y):
    y_refs = rest[:n_y]
    rest = rest[n_y:]
    if n_y:
        wo_ref, rest = rest[0], rest[1:]
    if final:
        gf_ref, o_ref = rest
    else:
        (o_ref,) = rest
    x = x_ref[0]
    if n_y:
        off = 0
        mix = None
        for y_ref in y_refs:
            k = y_ref.shape[2]
            part = jnp.dot(y_ref[0], wo_ref[off:off + k, :], preferred_element_type=F32)
            mix = part if mix is None else mix + part
            off += k
        x = x + (1.0 + mod_ref[0, 5]) * mix
    shift = mod_ref[0, 3 * which]
    scale = mod_ref[0, 3 * which + 1]
    gate = mod_ref[0, 3 * which + 2]
    h = _norm_mod(x, g_ref[...], shift, scale).astype(BF16)
    d_ff = w1_ref.shape[1]
    us = []
    for c in range(d_ff // fc):
        a = jnp.dot(h, w1_ref[:, c * fc:(c + 1) * fc], preferred_element_type=F32)
        b = jnp.dot(h, w3_ref[:, c * fc:(c + 1) * fc], preferred_element_type=F32)
        us.append((_silu(a) * b).astype(BF16))
    acc = jnp.dot(jnp.concatenate(us, axis=1), w2_ref[...], preferred_element_type=F32)
    y = x + 0.5 * (1.0 + gate) * acc
    if final:
        y = y * lax.rsqrt(jnp.mean(y * y, axis=-1, keepdims=True) + NORM_EPS) * gf_ref[...]
    o_ref[0] = y


def _mod_spec(mods, tm):
    _, n_mod, R, D = mods.shape
    if R == 1:
        return pl.BlockSpec((1, n_mod, 1, D), lambda b, i: (b, 0, 0, 0))
    return pl.BlockSpec((1, n_mod, tm, D), lambda b, i: (b, 0, i, 0))


def _layer_spec(shape, layer):
    nd = len(shape) - 1
    return pl.BlockSpec((None,) + tuple(shape[1:]), lambda *_: (layer,) + (0,) * nd, pipeline_mode=pl.Buffered(1))


def _ffn(x, mods, gain, w1, w3, w2, layer, which, final_gain=None, ys=(), w_out=None):
    B, L, D = x.shape
    d_ff = w1.shape[2]
    tm = _time_tile(L, FFN_ROWS)
    fc = MXU_COLS
    assert d_ff % fc == 0
    final = final_gain is not None
    ins = [x, mods, gain.reshape(1, D), w1, w3, w2]
    specs = [pl.BlockSpec((1, tm, D), lambda b, i: (b, i, 0)), _mod_spec(mods, tm), _const_spec((1, D)),
             _layer_spec(w1.shape, layer), _layer_spec(w3.shape, layer), _layer_spec(w2.shape, layer)]
    if ys:
        assert sum(y.shape[2] for y in ys) == w_out.shape[0]
        ins += list(ys) + [w_out]
        specs += [pl.BlockSpec((1, tm, y.shape[2]), lambda b, i: (b, i, 0)) for y in ys]
        specs.append(_const_spec(w_out.shape))
    if final:
        ins.append(final_gain.reshape(1, D))
        specs.append(_const_spec((1, D)))
    return pl.pallas_call(
        functools.partial(_ffn_kernel, which=which, fc=fc, final=final, n_y=len(ys)),
        out_shape=jax.ShapeDtypeStruct((B, L, D), F32),
        grid=(B, L // tm),
        in_specs=specs,
        out_specs=pl.BlockSpec((1, tm, D), lambda b, i: (b, i, 0)),
        compiler_params=_cparams(("parallel", "parallel")),
        name="ffn",
    )(*ins)


def _inproj_kernel(x_ref, mod_ref, g_ref, w_ref, *o_refs, splits):
    x = x_ref[0]
    h = _norm_mod(x, g_ref[...], mod_ref[0, 3], mod_ref[0, 4]).astype(BF16)
    off = 0
    for o_ref, n in zip(o_refs, splits):
        o_ref[0] = jnp.dot(h, w_ref[:, off:off + n], preferred_element_type=F32)
        off += n


def _inproj(x, mods, gain, w_in, splits):
    B, L, D = x.shape
    n = w_in.shape[1]
    assert sum(splits) == n
    tm = _time_tile(L, FFN_ROWS)
    return pl.pallas_call(
        functools.partial(_inproj_kernel, splits=splits),
        out_shape=[jax.ShapeDtypeStruct((B, L, s), F32) for s in splits],
        grid=(B, L // tm),
        in_specs=[pl.BlockSpec((1, tm, D), lambda b, i: (b, i, 0)), _mod_spec(mods, tm),
                  _const_spec((1, D)), _const_spec((D, n))],
        out_specs=[pl.BlockSpec((1, tm, s), lambda b, i: (b, i, 0)) for s in splits],
        compiler_params=_cparams(("parallel", "parallel")),
        name="mixer_in_proj",
    )(x, mods, gain.reshape(1, D), w_in)


def _s5_kernel(u_ref, h0_ref, perm_ref, permt_ref, wb_ref, wc_ref, a1r_ref, a1i_ref, d_ref, wg_ref,
               y_ref, hT_ref, carry_ref, *, tseg):
    t = pl.program_id(1)

    @pl.when(t == 0)
    def _():
        carry_ref[...] = h0_ref[...]

    u = jnp.concatenate([u_ref[k] for k in range(S5_SEQS)], axis=0)
    u = _dot3(perm_ref[...], u)
    nblk = u.shape[1] // LANES
    half = wb_ref.shape[2] // 2
    ys = []
    for j in range(nblk):
        cols = slice(j * half, (j + 1) * half)
        re_cols = slice(2 * j * half, (2 * j + 1) * half)
        im_cols = slice((2 * j + 1) * half, (2 * j + 2) * half)
        bu = _bdot(u[:, j * LANES:(j + 1) * LANES], wb_ref[j])
        ar = a1r_ref[:, cols]
        ai = a1i_ref[:, cols]
        sr = carry_ref[:, re_cols]
        si = carry_ref[:, im_cols]
        xr_rows, xi_rows = [], []
        for i in range(tseg):
            br = bu[S5_SEQS * i:S5_SEQS * (i + 1), :half]
            bi = bu[S5_SEQS * i:S5_SEQS * (i + 1), half:]
            sr, si = ar * sr - ai * si + br, ar * si + ai * sr + bi
            xr_rows.append(sr)
            xi_rows.append(si)
        carry_ref[:, re_cols] = sr
        carry_ref[:, im_cols] = si
        xr = jnp.concatenate(xr_rows, axis=0)
        xi = jnp.concatenate(xi_rows, axis=0)
        ys.append(_bdot(xr, wc_ref[j, :half, :]) + _bdot(xi, wc_ref[j, half:, :]))
    y = jnp.concatenate(ys, axis=1) + u * d_ref[...]
    y = jax.nn.gelu(y)
    y = (y * _sigmoid(_bdot(y, wg_ref[...]))).astype(BF16)
    y = jnp.dot(permt_ref[...], y, preferred_element_type=F32).astype(y_ref.dtype)
    for k in range(S5_SEQS):
        y_ref[k] = y[k * tseg:(k + 1) * tseg]
    hT_ref[...] = carry_ref[...]


def _s5_tables(lam_re, lam_im, log_dt, b_re, b_im, c_re, c_im):
    G, P = lam_re.shape
    H = b_re.shape[2]
    gpb = LANES // H
    nblk = G // gpb
    dt = jnp.exp(log_dt)[:, None]
    mag = jnp.exp(lam_re * dt)
    ab_re = mag * jnp.cos(lam_im * dt)
    ab_im = mag * jnp.sin(lam_im * dt)
    den = lam_re * lam_re + lam_im * lam_im
    z_re = ((ab_re - 1.0) * lam_re + ab_im * lam_im) / den
    z_im = (ab_im * lam_re - (ab_re - 1.0) * lam_im) / den
    bb_re = z_re[..., None] * b_re - z_im[..., None] * b_im
    bb_im = z_re[..., None] * b_im + z_im[..., None] * b_re
    eye = jnp.eye(gpb, dtype=F32)

    def blockdiag_in(bb):
        bb = bb.reshape(nblk, gpb, P, H)
        m = jnp.einsum('jgph,gk->jghkp', bb, eye)
        return m.reshape(nblk, gpb * H, gpb * P)

    def blockdiag_out(cc):
        cc = cc.reshape(nblk, gpb, H, P)
        m = jnp.einsum('jghp,gk->jgpkh', cc, eye)
        return m.reshape(nblk, gpb * P, gpb * H)

    wb = jnp.concatenate([blockdiag_in(bb_re), blockdiag_in(bb_im)], axis=2).astype(BF16)
    wc = jnp.concatenate([blockdiag_out(c_re), blockdiag_out(-c_im)], axis=1).astype(BF16)

    a1r = jnp.broadcast_to(ab_re.reshape(1, G * P), (S5_SEQS, G * P))
    a1i = jnp.broadcast_to(ab_im.reshape(1, G * P), (S5_SEQS, G * P))
    return wb, wc, a1r, a1i


def _s5_state_to_cols(h_re, h_im, nblk):
    B, G, P = h_re.shape
    gpb = G // nblk
    st = jnp.stack([h_re.reshape(B, nblk, gpb * P), h_im.reshape(B, nblk, gpb * P)], axis=2)
    return st.reshape(B, 2 * G * P)


def _s5_cols_to_state(cols, G, P, nblk):
    B = cols.shape[0]
    st = cols.reshape(B, nblk, 2, (G // nblk) * P)
    return st[:, :, 0].reshape(B, G, P), st[:, :, 1].reshape(B, G, P)


def _s5(u, h_re, h_im, lam_re, lam_im, log_dt, b_re, b_im, c_re, c_im, d_skip, w_glu_bf16):
    B, L, W = u.shape
    G, P = lam_re.shape
    assert B % S5_SEQS == 0
    tseg = _time_tile(L, S5_STEPS)
    T = S5_SEQS * tseg
    nblk = W // LANES
    tables = _s5_tables(lam_re, lam_im, log_dt, b_re, b_im, c_re, c_im)
    r = jnp.arange(T)
    perm = ((r % S5_SEQS) * tseg + r // S5_SEQS)[:, None] == jnp.arange(T)[None, :]
    perm = perm.astype(BF16)
    h0 = _s5_state_to_cols(h_re, h_im, nblk)
    ncol = 2 * G * P
    consts = (_tile3(perm), perm.T) + tables + (d_skip.reshape(1, W), w_glu_bf16)
    y, hT = pl.pallas_call(
        functools.partial(_s5_kernel, tseg=tseg),
        out_shape=[jax.ShapeDtypeStruct((B, L, W), BF16), jax.ShapeDtypeStruct((B, ncol), F32)],
        grid=(B // S5_SEQS, L // tseg),
        in_specs=[pl.BlockSpec((S5_SEQS, tseg, W), lambda b, t: (b, t, 0)),
                  pl.BlockSpec((S5_SEQS, ncol), lambda b, t: (b, 0))]
        + [_const_spec(c.shape) for c in consts],
        out_specs=[pl.BlockSpec((S5_SEQS, tseg, W), lambda b, t: (b, t, 0)),
                   pl.BlockSpec((S5_SEQS, ncol), lambda b, t: (b, 0))],
        scratch_shapes=[pltpu.VMEM((S5_SEQS, ncol), F32)],
        compiler_params=_cparams(("parallel", "arbitrary")),
        name="s5_mixer",
    )(u, h0, *consts)
    hr, hi = _s5_cols_to_state(hT, G, P, nblk)
    return y, hr, hi


def _stack_heads(x, lane_lo):
    zero = jnp.zeros_like(x)
    return jnp.concatenate([jnp.where(lane_lo, x, zero), jnp.where(lane_lo, zero, x)], axis=0)


def _head_sum(x, hm2):
    nblk = x.shape[1] // LANES
    R = x.shape[0]
    xs = jnp.concatenate([x[:, j * LANES:(j + 1) * LANES] for j in range(nblk)], axis=0)
    s = _dot2(xs, hm2)
    return jnp.concatenate([s[j * R:(j + 1) * R] for j in range(nblk)], axis=1)


def _rwkv_kernel(p_ref, sp_ref, s0_ref, mu_ref, w0_ref, ww2_ref, a0_ref, wa2_ref, wg2_ref, kk_ref, ka_ref,
                 rk_ref, lng_ref, lnb_ref, hm_ref, tri_ref, y_ref, sh_ref, sT_ref, prev_ref, st_ref, *, width):
    t = pl.program_id(1)
    W = width
    NB, C, _ = p_ref.shape

    @pl.when(t == 0)
    def _():
        prev_ref[...] = sp_ref[...]
        st_ref[...] = s0_ref[...]

    rows = lax.broadcasted_iota(jnp.int32, (C, 1), 0)
    ps = []
    for n in range(NB):
        p = p_ref[n]
        p_prev = jnp.where(rows == 0, prev_ref[n], pltpu.roll(p, 1, axis=0))
        prev_ref[n] = p[C - 1:C, :]
        sh_ref[n] = p[C - 1:C, :]
        ps.append(p + (p_prev - p) * mu_ref[...])
    ps = jnp.concatenate(ps, axis=0)
    r = ps[:, :W]
    k = ps[:, W:2 * W]
    v = ps[:, 2 * W:3 * W]
    wa_low = ps[:, 3 * W:3 * W + LANES]
    g_low = ps[:, 3 * W + LANES:]
    xw = w0_ref[...] + _bdot(jnp.tanh(wa_low), ww2_ref[...])
    lw = -LOG2E * jnp.exp(-_softplus(-xw) - 0.5)
    a = _sigmoid(a0_ref[...] + _bdot(wa_low, wa2_ref[...]))
    g = _bdot(_sigmoid(g_low), wg2_ref[...])
    hm = hm_ref[...]
    kk = k * kk_ref[...]
    kk = kk / jnp.maximum(jnp.sqrt(_head_sum(kk * kk, hm)), 1e-12)
    kmod = k * (1.0 + (a - 1.0) * ka_ref[...])
    bvec = kk * a

    cum = _dot3(tri_ref[...], lw)
    e_in = jnp.exp2(cum)
    e_ex = jnp.exp2(cum - lw)
    e_neg = jnp.exp2(-cum)
    cum_last = [cum[(n + 1) * C - 1:(n + 1) * C, :] for n in range(NB)]
    e_end = jnp.exp2(jnp.concatenate([jnp.broadcast_to(c, (C, W)) for c in cum_last], axis=0) - cum)
    w_end = [jnp.exp2(c) for c in cum_last]
    at = (-kk * e_ex).astype(BF16)
    rt = (r * e_in).astype(BF16)
    bt = (bvec * e_neg).astype(BF16)
    kt = (kmod * e_neg).astype(BF16)
    bw = (bvec * e_end).astype(BF16)
    kw = (kmod * e_end).astype(BF16)
    vb = v.astype(BF16)

    C2 = 2 * C
    si = lax.broadcasted_iota(jnp.int32, (C, C2), 0)
    sj = lax.broadcasted_iota(jnp.int32, (C, C2), 1) % C
    strict = si > sj
    lower = si >= sj
    lane_lo = lax.broadcasted_iota(jnp.int32, (1, LANES), 1) < RWKV_HEAD
    side_lo = lax.broadcasted_iota(jnp.int32, (1, C2), 1) < C
    npair = W // LANES
    chains = [(n, j) for n in range(NB) for j in range(npair)]
    x0, vst, bkw, l_ab, a_ak, a_r = {}, {}, {}, {}, {}, {}
    for c in chains:
        n, j = c
        pair = lambda x: x[n * C:(n + 1) * C, j * LANES:(j + 1) * LANES]
        ar = jnp.concatenate([pair(at), pair(rt)], axis=0)
        bk_st = jnp.concatenate([_stack_heads(pair(bt), lane_lo), _stack_heads(pair(kt), lane_lo)], axis=0)
        m = _bdot_nt(ar, bk_st)
        x0[c] = _bdot_nt(ar, st_ref[n, j])
        vst[c] = _stack_heads(pair(vb), lane_lo)
        bkw[c] = jnp.concatenate([_stack_heads(pair(bw), lane_lo), _stack_heads(pair(kw), lane_lo)], axis=0)
        l_ab[c] = jnp.where(strict, m[:C, :C2], 0.0)
        a_ak[c] = jnp.where(strict, m[:C, C2:], 0.0).astype(BF16)
        a_r[c] = jnp.concatenate([jnp.where(lower, m[C:, :C2], 0.0), jnp.where(lower, m[C:, C2:], 0.0)],
                                 axis=1).astype(BF16)
    def split(x):
        hi = x.astype(BF16)
        return hi, (x - hi.astype(F32)).astype(BF16)

    def lhs3(x):
        hi, lo = split(x)
        return jnp.concatenate([hi, lo, hi], axis=1)

    def rhs3(x, mask):
        hi, lo = split(x)
        hi = _stack_heads(hi, mask)
        return jnp.concatenate([hi, hi, _stack_heads(lo, mask)], axis=0)

    blk_t = si // RWKV_BASE
    blk_s = sj // RWKV_BASE
    diag8 = blk_t == blk_s
    l_q = {c: l_ab[c].astype(BF16) for c in chains}
    zero = jnp.zeros((C, C2), BF16)
    tinv = {c: jnp.where(si == sj, 1.0, 0.0) + jnp.where(diag8, l_q[c], zero).astype(F32) for c in chains}
    pw = {c: jnp.where(diag8, l_q[c], zero) for c in chains}
    for _ in range(int(math.log2(RWKV_BASE)) - 1):
        for c in chains:
            pw[c] = _bdot(pw[c], _stack_heads(pw[c].astype(BF16), side_lo))
        for c in chains:
            tinv[c] = tinv[c] + _bdot(pw[c], _stack_heads(tinv[c].astype(BF16), side_lo))

    def rhs2(x, mask):
        hi, lo = split(x)
        return jnp.concatenate([_stack_heads(hi, mask), _stack_heads(lo, mask)], axis=0)

    b = RWKV_BASE
    while b < C:
        lower_left = ((si // (2 * b)) == (sj // (2 * b))) & (((si // b) % 2) == 1) & (((sj // b) % 2) == 0)
        l_b = {c: jnp.where(lower_left, l_q[c], zero) for c in chains}
        if b == RWKV_BASE:
            lt = {c: _bdot(l_b[c], _stack_heads(tinv[c].astype(BF16), side_lo)) for c in chains}
            for c in chains:
                tinv[c] = tinv[c] + _bdot(tinv[c], _stack_heads(lt[c].astype(BF16), side_lo))
        else:
            lt = {c: jnp.dot(jnp.concatenate([l_b[c], l_b[c]], axis=1), rhs2(tinv[c], side_lo),
                             preferred_element_type=F32) for c in chains}
            for c in chains:
                tinv[c] = tinv[c] + jnp.dot(lhs3(tinv[c]), rhs3(lt[c], side_lo), preferred_element_type=F32)
        b *= 2
    z = {c: x0[c][:C] + jnp.dot(a_ak[c], vst[c], preferred_element_type=F32) for c in chains}
    u = {c: jnp.dot(lhs3(tinv[c]), rhs3(z[c], lane_lo), preferred_element_type=F32) for c in chains}
    uv = {c: jnp.concatenate([_stack_heads(u[c].astype(BF16), lane_lo), vst[c]], axis=0) for c in chains}
    y_p = {c: x0[c][C:] + jnp.dot(a_r[c], uv[c], preferred_element_type=F32) for c in chains}
    for c in chains:
        n, j = c
        st_ref[n, j] = st_ref[n, j] * w_end[n][:, j * LANES:(j + 1) * LANES] + _bdot_tn(uv[c], bkw[c])
    y = jnp.concatenate([jnp.concatenate([y_p[(n, j)] for j in range(npair)], axis=1)
                         for n in range(NB)], axis=0)
    inv_n = 1.0 / RWKV_HEAD
    mean = _head_sum(y, hm) * inv_n
    yc = y - mean
    var = _head_sum(yc * yc, hm) * inv_n
    yn = yc * lax.rsqrt(var + RWKV_GN_EPS) * lng_ref[...] + lnb_ref[...]
    bonus = _head_sum(r * kmod * rk_ref[...], hm) * v
    out = ((yn + bonus) * g).astype(y_ref.dtype)
    for n in range(NB):
        y_ref[n] = out[n * C:(n + 1) * C]

    @pl.when(t == pl.num_programs(1) - 1)
    def _():
        sT_ref[...] = st_ref[...]


def _rwkv(p, shift_prev, s0, mu, w0, w_w2, a0, w_a2, w_g2, k_k, k_a, r_k, lnx_g, lnx_b):
    B, L, cols = p.shape
    H, N = s0.shape[1], s0.shape[2]
    W = H * N
    C = _time_tile(L, 64)
    assert 1 << int(math.log2(C)) == C and 2 * N == LANES
    npair = H // 2
    s0p = s0.reshape(B, npair, 2, N, N)
    z = jnp.zeros_like(s0p[:, :, 0])
    s0bd = jnp.concatenate([jnp.concatenate([s0p[:, :, 0], z], axis=-1),
                            jnp.concatenate([z, s0p[:, :, 1]], axis=-1)], axis=-2)
    lo = w_w2.shape[0]
    ww2p = jnp.concatenate([w_w2, jnp.zeros((LANES - lo, W), F32)], axis=0).astype(BF16)
    wa2p = jnp.concatenate([jnp.zeros((lo, W), F32), w_a2], axis=0).astype(BF16)
    hid = jnp.arange(LANES) // N
    hm = (hid[:, None] == hid[None, :]).astype(BF16)
    hm2 = jnp.concatenate([hm, hm], axis=0)
    nb = RWKV_SEQS if B % RWKV_SEQS == 0 else 1
    tr = jnp.arange(nb * C)
    tri = ((tr[:, None] >= tr[None, :]) & (tr[:, None] // C == tr[None, :] // C)).astype(BF16)
    row = lambda x: x.reshape(1, -1)
    y, sh, sT = pl.pallas_call(
        functools.partial(_rwkv_kernel, width=W),
        out_shape=[jax.ShapeDtypeStruct((B, L, W), BF16),
                   jax.ShapeDtypeStruct((B, 1, cols), F32),
                   jax.ShapeDtypeStruct((B, npair, LANES, LANES), F32)],
        grid=(B // nb, L // C),
        in_specs=[pl.BlockSpec((nb, C, cols), lambda b, t: (b, t, 0)),
                  pl.BlockSpec((nb, 1, cols), lambda b, t: (b, 0, 0)),
                  pl.BlockSpec((nb, npair, LANES, LANES), lambda b, t: (b, 0, 0, 0)),
                  _const_spec((1, cols)), _const_spec((1, W)), _const_spec((LANES, W)), _const_spec((1, W)),
                  _const_spec((LANES, W)), _const_spec((LANES, W)), _const_spec((1, W)), _const_spec((1, W)),
                  _const_spec((1, W)), _const_spec((1, W)), _const_spec((1, W)), _const_spec((2 * LANES, LANES)),
                  _const_spec((nb * C, 3 * nb * C))],
        out_specs=[pl.BlockSpec((nb, C, W), lambda b, t: (b, t, 0)),
                   pl.BlockSpec((nb, 1, cols), lambda b, t: (b, 0, 0)),
                   pl.BlockSpec((nb, npair, LANES, LANES), lambda b, t: (b, 0, 0, 0))],
        scratch_shapes=[pltpu.VMEM((nb, 1, cols), F32), pltpu.VMEM((nb, npair, LANES, LANES), F32)],
        compiler_params=_cparams(("parallel", "arbitrary")),
        name="rwkv7_mixer",
    )(p, shift_prev.reshape(B, 1, cols), s0bd, row(mu), row(w0), ww2p, row(a0), wa2p, w_g2.astype(BF16),
      row(k_k), row(k_a), row(r_k), row(lnx_g), row(lnx_b), hm2, _tile3(tri))
    sT = sT.reshape(B, npair, 2, N, 2, N)
    s_new = jnp.stack([sT[:, :, 0, :, 0, :], sT[:, :, 1, :, 1, :]], axis=2).reshape(B, H, N, N)
    return y, sh[:, 0], s_new


def _hgrn_segment_matrix(C):
    t = jnp.arange(C)[:, None]
    r = jnp.arange(C)[None, :]
    blocks = [r <= t]
    h = C // 2
    while h >= 1:
        mid = (t // (2 * h)) * (2 * h) + h - 1
        is_q = (t & h) != 0
        blocks.append(jnp.where(is_q, (r > mid) & (r <= t), (r > t) & (r <= mid)))
        h //= 2
    return jnp.concatenate(blocks, axis=0).astype(BF16)


def _hgrn_kernel(p_ref, s0_ref, llb_ref, l1lb_ref, omlb_ref, gn_ref, mseg_ref, y_ref, sT_ref, st_ref, *, width):
    t = pl.program_id(1)
    W = width
    H = W // HGRN_HEAD
    NB, C, _ = p_ref.shape

    @pl.when(t == 0)
    def _():
        for n in range(NB):
            for h in range(H):
                st_ref[n, h] = s0_ref[n, h].T

    p = jnp.concatenate([p_ref[n] for n in range(NB)], axis=0)
    q = _silu(p[:, :W])
    f = p[:, W:2 * W]
    v = p[:, 2 * W:3 * W]
    g = p[:, 3 * W:]
    ef = jnp.exp(-jnp.abs(f))
    lsig = jnp.minimum(f, 0.0) - jnp.log(1.0 + ef)
    c1 = llb_ref[...]
    c2 = l1lb_ref[...] + lsig
    log_f = jnp.maximum(c1, c2) + jnp.log(1.0 + jnp.exp(-jnp.abs(c1 - c2)))
    kv = omlb_ref[...] * (jnp.where(f >= 0.0, ef, 1.0) / (1.0 + ef))
    log2_f = log_f * LOG2E
    d_all = [_dot3(mseg_ref[...], log2_f[n * C:(n + 1) * C]) for n in range(NB)]
    b = jnp.concatenate([d[:C] for d in d_all], axis=0)
    b_last = [d[C - 1:C, :] for d in d_all]
    q_b = q.astype(BF16)
    kv_b = kv.astype(BF16)
    qe = jnp.exp2(b).astype(BF16) * q_b
    ke = jnp.exp2(jnp.concatenate([jnp.broadcast_to(bl, (C, W)) for bl in b_last], axis=0) - b).astype(BF16) * kv_b
    e_last = [jnp.exp2(bl) for bl in b_last]
    qk = q * kv
    vb = v.astype(BF16)
    rowi = lax.broadcasted_iota(jnp.int32, (NB * C, 1), 0)
    ti = lax.broadcasted_iota(jnp.int32, (C, C), 0)
    tj = lax.broadcasted_iota(jnp.int32, (C, C), 1)
    zs, masks = [], []
    hsz = C // 2
    lvl = 1
    while hsz >= 1:
        is_q = (rowi & hsz) != 0
        seg = jnp.concatenate([d[lvl * C:(lvl + 1) * C] for d in d_all], axis=0)
        zs.append(jnp.exp2(seg).astype(BF16) * jnp.where(is_q, q_b, kv_b))
        masks.append(((ti & hsz) != 0) & ((tj & hsz) == 0) & ((ti // (2 * hsz)) == (tj // (2 * hsz))))
        hsz //= 2
        lvl += 1
    chains = [(n, h) for n in range(NB) for h in range(H)]
    blk = lambda x, c: x[c[0] * C:(c[0] + 1) * C, c[1] * HGRN_HEAD:(c[1] + 1) * HGRN_HEAD]
    att = {}
    for c in chains:
        for z, mask in zip(zs, masks):
            zh = blk(z, c)
            term = jnp.where(mask, _bdot_nt(zh, zh), 0.0)
            att[c] = term if c not in att else att[c] + term
    o = {c: (_bdot(att[c], blk(vb, c)) + _bdot_nt(blk(qe, c), st_ref[c[0], c[1]])
             + jnp.sum(blk(qk, c), axis=-1, keepdims=True) * blk(v, c)) for c in chains}
    for c in chains:
        n, h = c
        st_ref[n, h] = (st_ref[n, h] * e_last[n][:, h * HGRN_HEAD:(h + 1) * HGRN_HEAD]
                        + _bdot_tn(blk(vb, c), blk(ke, c)))
    for c in chains:
        o[c] = o[c] * lax.rsqrt(jnp.mean(o[c] * o[c], axis=-1, keepdims=True) + NORM_EPS) * gn_ref[...]
    o = jnp.concatenate([jnp.concatenate([o[(n, h)] for h in range(H)], axis=1) for n in range(NB)], axis=0)
    out = (o * _silu(g)).astype(y_ref.dtype)
    for n in range(NB):
        y_ref[n] = out[n * C:(n + 1) * C]

    @pl.when(t == pl.num_programs(1) - 1)
    def _():
        for n in range(NB):
            for h in range(H):
                sT_ref[n, h] = st_ref[n, h].T


def _hgrn(p, s0, lb, gnorm):
    B, L, cols = p.shape
    W = cols // 4
    H = W // HGRN_HEAD
    C = _time_tile(L, 64)
    assert 1 << int(math.log2(C)) == C
    mseg = _tile3(_hgrn_segment_matrix(C))
    nb = HGRN_SEQS if B % HGRN_SEQS == 0 else 1
    row = lambda x: x.reshape(1, -1)
    y, sT = pl.pallas_call(
        functools.partial(_hgrn_kernel, width=W),
        out_shape=[jax.ShapeDtypeStruct((B, L, W), BF16),
                   jax.ShapeDtypeStruct((B, H, HGRN_HEAD, HGRN_HEAD), F32)],
        grid=(B // nb, L // C),
        in_specs=[pl.BlockSpec((nb, C, cols), lambda b, t: (b, t, 0)),
                  pl.BlockSpec((nb, H, HGRN_HEAD, HGRN_HEAD), lambda b, t: (b, 0, 0, 0)),
                  _const_spec((1, W)), _const_spec((1, W)), _const_spec((1, W)), _const_spec((1, HGRN_HEAD)),
                  _const_spec(mseg.shape)],
        out_specs=[pl.BlockSpec((nb, C, W), lambda b, t: (b, t, 0)),
                   pl.BlockSpec((nb, H, HGRN_HEAD, HGRN_HEAD), lambda b, t: (b, 0, 0, 0))],
        scratch_shapes=[pltpu.VMEM((nb, H, HGRN_HEAD, HGRN_HEAD), F32)],
        compiler_params=_cparams(("parallel", "arbitrary")),
        name="hgrn2_mixer",
    )(p, s0, row(jnp.log(lb)), row(jnp.log1p(-lb)), row(1.0 - lb), row(gnorm), mseg)
    return y, sT


def _trunk(x, mods_all, s5_re, s5_im, rwkv_s, rwkv_shift, hgrn_s, P, Pb):
    depth = P['norm_mix'].shape[0]
    lb_all = jax.nn.softmax(P['hgrn_lower_bounds'], axis=0)
    lb_all = jnp.cumsum(lb_all, axis=0) - lb_all[0]
    s5_width = P['s5_w_glu'].shape[1]
    n_re, n_im, n_rw, n_sh, n_hg = [], [], [], [], []
    B, L, D = x.shape
    flat = L < FLAT_BELOW
    if flat:
        x = x.reshape(1, B * L, D)
        mods_all = jnp.repeat(mods_all, L, axis=1).transpose(0, 2, 1, 3)[:, None]
    else:
        mods_all = mods_all[:, :, :, None, :]
    seq = lambda a: a.reshape(B, L, a.shape[-1])
    tok = lambda a: a.reshape(x.shape[0], x.shape[1], a.shape[-1])
    for l in range(depth):
        mods = mods_all[l]
        x = _ffn(x, mods, P['norm_ffn1'][l], Pb['ffn1_w1'], Pb['ffn1_w3'], Pb['ffn1_w2'], l, 0)
        i = l // 2
        if l % 2 == 0:
            n_in = Pb['ab_w_in'].shape[2]
            u, pr = _inproj(x, mods, P['norm_mix'][l], Pb['ab_w_in'][i], (s5_width, n_in - s5_width))
            ya, hr, hi = _s5(seq(u), s5_re[i], s5_im[i], P['s5_lam_re'][i], P['s5_lam_im'][i],
                             P['s5_log_dt'][i], P['s5_b_re'][i], P['s5_b_im'][i], P['s5_c_re'][i],
                             P['s5_c_im'][i], P['s5_d'][i], Pb['s5_w_glu'][i])
            yb, sh_new, rw_new = _rwkv(seq(pr), rwkv_shift[i], rwkv_s[i], P['rwkv_mu'][i], P['rwkv_w0'][i],
                                       P['rwkv_w_w2'][i], P['rwkv_a0'][i], P['rwkv_w_a2'][i],
                                       P['rwkv_w_g2'][i], P['rwkv_k_k'][i], P['rwkv_k_a'][i],
                                       P['rwkv_r_k'][i], P['rwkv_lnx_g'][i], P['rwkv_lnx_b'][i])
            ys, w_out = (tok(ya), tok(yb)), Pb['ab_w_out'][i]
            n_re.append(hr)
            n_im.append(hi)
            n_rw.append(rw_new)
            n_sh.append(sh_new)
        else:
            (pc,) = _inproj(x, mods, P['norm_mix'][l], Pb['c_w_in'][i], (Pb['c_w_in'].shape[2],))
            yc, hg_new = _hgrn(seq(pc), hgrn_s[i], lb_all[l], P['hgrn_gnorm'][i])
            ys, w_out = (tok(yc),), Pb['c_w_out'][i]
            n_hg.append(hg_new)
        fin = P['final_norm'] if l == depth - 1 else None
        x = _ffn(x, mods, P['norm_ffn2'][l], Pb['ffn2_w1'], Pb['ffn2_w3'], Pb['ffn2_w2'], l, 2,
                 final_gain=fin, ys=ys, w_out=w_out)
    return (x.reshape(B, L, D), jnp.stack(n_re), jnp.stack(n_im), jnp.stack(n_rw), jnp.stack(n_sh),
            jnp.stack(n_hg))


def kernel(x_prompt, x_sample, state_s5_re, state_s5_im, state_rwkv, state_rwkv_shift, state_hgrn, c_prompt, c_sample, w_ada, b_ada, norm_ffn1, norm_mix, norm_ffn2, ffn1_w1, ffn1_w3, ffn1_w2, ffn2_w1, ffn2_w3, ffn2_w2, ab_w_in, ab_w_out, s5_lam_re, s5_lam_im, s5_log_dt, s5_b_re, s5_b_im, s5_c_re, s5_c_im, s5_d, s5_w_glu, rwkv_mu, rwkv_w0, rwkv_w_w2, rwkv_a0, rwkv_w_a2, rwkv_w_g2, rwkv_k_k, rwkv_k_a, rwkv_r_k, rwkv_lnx_g, rwkv_lnx_b, c_w_in, c_w_out, hgrn_lower_bounds, hgrn_gnorm, final_norm):
    P = dict(norm_ffn1=norm_ffn1, norm_mix=norm_mix, norm_ffn2=norm_ffn2,
             s5_lam_re=s5_lam_re, s5_lam_im=s5_lam_im, s5_log_dt=s5_log_dt, s5_b_re=s5_b_re, s5_b_im=s5_b_im,
             s5_c_re=s5_c_re, s5_c_im=s5_c_im, s5_d=s5_d, s5_w_glu=s5_w_glu, rwkv_mu=rwkv_mu,
             rwkv_w0=rwkv_w0, rwkv_w_w2=rwkv_w_w2, rwkv_a0=rwkv_a0, rwkv_w_a2=rwkv_w_a2, rwkv_w_g2=rwkv_w_g2,
             rwkv_k_k=rwkv_k_k, rwkv_k_a=rwkv_k_a, rwkv_r_k=rwkv_r_k, rwkv_lnx_g=rwkv_lnx_g,
             rwkv_lnx_b=rwkv_lnx_b, hgrn_lower_bounds=hgrn_lower_bounds, hgrn_gnorm=hgrn_gnorm,
             final_norm=final_norm)
    Pb = {n: w.astype(BF16) for n, w in dict(
        ffn1_w1=ffn1_w1, ffn1_w3=ffn1_w3, ffn1_w2=ffn1_w2, ffn2_w1=ffn2_w1, ffn2_w3=ffn2_w3, ffn2_w2=ffn2_w2,
        ab_w_in=ab_w_in, ab_w_out=ab_w_out, c_w_in=c_w_in, c_w_out=c_w_out, s5_w_glu=s5_w_glu).items()}
    depth, d = norm_mix.shape
    bp, bs = x_prompt.shape[0], x_sample.shape[0]
    mods = _ada(jnp.concatenate([c_prompt, c_sample], axis=0), w_ada, b_ada)
    mods = mods.reshape(depth, bp + bs, N_MOD, d)
    dt = x_prompt.dtype
    zeros_like_b = lambda s: jnp.zeros((s.shape[0], bp) + s.shape[2:], dt)
    outs_p = _trunk(x_prompt, mods[:, :bp], zeros_like_b(state_s5_re), zeros_like_b(state_s5_im),
                    zeros_like_b(state_rwkv), zeros_like_b(state_rwkv_shift), zeros_like_b(state_hgrn), P, Pb)
    outs_s = _trunk(x_sample, mods[:, bp:], state_s5_re, state_s5_im, state_rwkv, state_rwkv_shift,
                    state_hgrn, P, Pb)
    return (outs_p[0], outs_s[0]) + tuple(outs_p[1:]) + tuple(outs_s[1:])
```

```python
import functools
import math

import jax
import jax.numpy as jnp
from jax import lax
from jax.experimental import pallas as pl
from jax.experimental.pallas import tpu as pltpu

F32 = jnp.float32
BF16 = jnp.bfloat16

NORM_EPS = 1e-6
LOG2E = 1.4426950408889634
RWKV_GN_EPS = 64e-5
N_MOD = 9
RWKV_HEAD = 64
RWKV_BASE = 8
HGRN_HEAD = 128
LANES = 128
MXU_COLS = 256
S5_SEQS = 8
S5_STEPS = 64
RWKV_SEQS = 4
HGRN_SEQS = 4
VMEM_LIMIT = 56 * 1024 * 1024
FFN_ROWS = 1024
FLAT_BELOW = 64


def _bdot(a, b):
    return jnp.dot(a.astype(BF16), b.astype(BF16), preferred_element_type=F32)


def _bdot_nt(a, b):
    return lax.dot_general(a.astype(BF16), b.astype(BF16), (((1,), (1,)), ((), ())),
                           preferred_element_type=F32)


def _bdot_tn(a, b):
    return lax.dot_general(a.astype(BF16), b.astype(BF16), (((0,), (0,)), ((), ())),
                           preferred_element_type=F32)


def _dot2(a, b2):
    hi = a.astype(BF16)
    lo = (a - hi.astype(F32)).astype(BF16)
    return jnp.dot(jnp.concatenate([hi, lo], axis=1), b2, preferred_element_type=F32)


def _dot3(m3, x):
    hi = x.astype(BF16)
    r1 = x - hi.astype(F32)
    mid = r1.astype(BF16)
    lo = (r1 - mid.astype(F32)).astype(BF16)
    return jnp.dot(m3, jnp.concatenate([hi, mid, lo], axis=0), preferred_element_type=F32)


def _tile3(m):
    return jnp.concatenate([m, m, m], axis=1)


def _sigmoid(x):
    return 1.0 / (1.0 + jnp.exp(-x))


def _silu(x):
    return x * _sigmoid(x)


def _softplus(x):
    return jnp.maximum(x, 0.0) + jnp.log(1.0 + jnp.exp(-jnp.abs(x)))


def _cparams(sem):
    return pltpu.CompilerParams(dimension_semantics=sem, vmem_limit_bytes=VMEM_LIMIT)


def _const_spec(shape):
    nd = len(shape)
    return pl.BlockSpec(shape, lambda *_: (0,) * nd, pipeline_mode=pl.Buffered(1))


def _time_tile(L, want):
    t = min(L, want)
    assert L % t == 0 and t % 8 == 0
    return t


def _ada_kernel(c_ref, w_ref, b_ref, o_ref):
    cs = _silu(c_ref[...])
    o_ref[0] = _bdot(cs, w_ref[0]) + b_ref[0]


def _ada(c_all, w_ada, b_ada):
    depth, d, n = w_ada.shape
    nb = c_all.shape[0]
    tn = 1024
    return pl.pallas_call(
        _ada_kernel,
        out_shape=jax.ShapeDtypeStruct((depth, nb, n), F32),
        grid=(depth, n // tn),
        in_specs=[pl.BlockSpec((nb, d), lambda l, j: (0, 0)),
                  pl.BlockSpec((1, d, tn), lambda l, j: (l, 0, j)),
                  pl.BlockSpec((1, 1, tn), lambda l, j: (l, 0, j))],
        out_specs=pl.BlockSpec((1, nb, tn), lambda l, j: (l, 0, j)),
        compiler_params=_cparams(("arbitrary", "arbitrary")),
        name="ada_mod",
    )(c_all, w_ada, b_ada.reshape(depth, 1, n))


def _norm_mod(x, gain, shift, scale):
    y = x * lax.rsqrt(jnp.mean(x * x, axis=-1, keepdims=True) + NORM_EPS) * gain
    return y * (1.0 + scale) + shift


def _ffn_kernel(x_ref, mod_ref, g_ref, w1_ref, w3_ref, w2_ref, *rest, which, fc, final, n_y):
    y_refs = rest[:n_y]
    rest = rest[n_y:]
    if n_y:
        wo_ref, rest = rest[0], rest[1:]
    if final:
        gf_ref, o_ref = rest
    else:
        (o_ref,) = rest
    x = x_ref[0]
    if n_y:
        off = 0
        mix = None
        for y_ref in y_refs:
            k = y_ref.shape[2]
            part = jnp.dot(y_ref[0], wo_ref[off:off + k, :], preferred_element_type=F32)
            mix = part if mix is None else mix + part
            off += k
        x = x + (1.0 + _mod_rows(mod_ref, 5, x.shape[0])) * mix
    shift = _mod_rows(mod_ref, 3 * which, x.shape[0])
    scale = _mod_rows(mod_ref, 3 * which + 1, x.shape[0])
    gate = _mod_rows(mod_ref, 3 * which + 2, x.shape[0])
    h = _norm_mod(x, g_ref[...], shift, scale).astype(BF16)
    d_ff = w1_ref.shape[1]
    us = []
    for c in range(d_ff // fc):
        a = jnp.dot(h, w1_ref[:, c * fc:(c + 1) * fc], preferred_element_type=F32)
        b = jnp.dot(h, w3_ref[:, c * fc:(c + 1) * fc], preferred_element_type=F32)
        us.append((_silu(a) * b).astype(BF16))
    acc = jnp.dot(jnp.concatenate(us, axis=1), w2_ref[...], preferred_element_type=F32)
    y = x + 0.5 * (1.0 + gate) * acc
    if final:
        y = y * lax.rsqrt(jnp.mean(y * y, axis=-1, keepdims=True) + NORM_EPS) * gf_ref[...]
    o_ref[0] = y


def _mod_rows(mod_ref, k, rows):
    groups = mod_ref.shape[2]
    if groups == 1:
        return mod_ref[0, k]
    run = rows // groups
    return jnp.concatenate([jnp.broadcast_to(mod_ref[0, k, g:g + 1, :], (run, mod_ref.shape[3]))
                            for g in range(groups)], axis=0)


def _mod_spec(mods, tm):
    _, n_mod, G, D = mods.shape
    if G == 1:
        return pl.BlockSpec((1, n_mod, 1, D), lambda b, i: (b, 0, 0, 0))
    assert mods.shape[0] == 1 and tm % G == 0 and (tm // G) % 8 == 0
    return pl.BlockSpec((1, n_mod, G, D), lambda b, i: (0, 0, 0, 0))


def _layer_spec(shape, layer):
    nd = len(shape) - 1
    return pl.BlockSpec((None,) + tuple(shape[1:]), lambda *_: (layer,) + (0,) * nd, pipeline_mode=pl.Buffered(1))


def _ffn(x, mods, gain, w1, w3, w2, layer, which, final_gain=None, ys=(), w_out=None):
    B, L, D = x.shape
    d_ff = w1.shape[2]
    tm = _time_tile(L, FFN_ROWS)
    fc = MXU_COLS
    assert d_ff % fc == 0
    final = final_gain is not None
    ins = [x, mods, gain.reshape(1, D), w1, w3, w2]
    specs = [pl.BlockSpec((1, tm, D), lambda b, i: (b, i, 0)), _mod_spec(mods, tm), _const_spec((1, D)),
             _layer_spec(w1.shape, layer), _layer_spec(w3.shape, layer), _layer_spec(w2.shape, layer)]
    if ys:
        assert sum(y.shape[2] for y in ys) == w_out.shape[0]
        ins += list(ys) + [w_out]
        specs += [pl.BlockSpec((1, tm, y.shape[2]), lambda b, i: (b, i, 0)) for y in ys]
        specs.append(_const_spec(w_out.shape))
    if final:
        ins.append(final_gain.reshape(1, D))
        specs.append(_const_spec((1, D)))
    return pl.pallas_call(
        functools.partial(_ffn_kernel, which=which, fc=fc, final=final, n_y=len(ys)),
        out_shape=jax.ShapeDtypeStruct((B, L, D), F32),
        grid=(B, L // tm),
        in_specs=specs,
        out_specs=pl.BlockSpec((1, tm, D), lambda b, i: (b, i, 0)),
        compiler_params=_cparams(("parallel", "parallel")),
        name="ffn",
    )(*ins)


def _inproj_kernel(x_ref, mod_ref, g_ref, w_ref, *o_refs, splits):
    x = x_ref[0]
    h = _norm_mod(x, g_ref[...], _mod_rows(mod_ref, 3, x.shape[0]), _mod_rows(mod_ref, 4, x.shape[0])).astype(BF16)
    off = 0
    for o_ref, n in zip(o_refs, splits):
        o_ref[0] = jnp.dot(h, w_ref[:, off:off + n], preferred_element_type=F32)
        off += n


def _inproj(x, mods, gain, w_in, splits):
    B, L, D = x.shape
    n = w_in.shape[1]
    assert sum(splits) == n
    tm = _time_tile(L, FFN_ROWS)
    return pl.pallas_call(
        functools.partial(_inproj_kernel, splits=splits),
        out_shape=[jax.ShapeDtypeStruct((B, L, s), F32) for s in splits],
        grid=(B, L // tm),
        in_specs=[pl.BlockSpec((1, tm, D), lambda b, i: (b, i, 0)), _mod_spec(mods, tm),
                  _const_spec((1, D)), _const_spec((D, n))],
        out_specs=[pl.BlockSpec((1, tm, s), lambda b, i: (b, i, 0)) for s in splits],
        compiler_params=_cparams(("parallel", "parallel")),
        name="mixer_in_proj",
    )(x, mods, gain.reshape(1, D), w_in)


def _s5_kernel(u_ref, h0_ref, perm_ref, permt_ref, wb_ref, wc_ref, a1r_ref, a1i_ref, d_ref, wg_ref,
               y_ref, hT_ref, carry_ref, *, tseg):
    t = pl.program_id(1)

    @pl.when(t == 0)
    def _():
        carry_ref[...] = h0_ref[...]

    u = jnp.concatenate([u_ref[k] for k in range(S5_SEQS)], axis=0)
    u = _dot3(perm_ref[...], u)
    nblk = u.shape[1] // LANES
    half = wb_ref.shape[2] // 2
    ys = []
    for j in range(nblk):
        cols = slice(j * half, (j + 1) * half)
        re_cols = slice(2 * j * half, (2 * j + 1) * half)
        im_cols = slice((2 * j + 1) * half, (2 * j + 2) * half)
        bu = _bdot(u[:, j * LANES:(j + 1) * LANES], wb_ref[j])
        ar = a1r_ref[:, cols]
        ai = a1i_ref[:, cols]
        sr = carry_ref[:, re_cols]
        si = carry_ref[:, im_cols]
        xr_rows, xi_rows = [], []
        for i in range(tseg):
            br = bu[S5_SEQS * i:S5_SEQS * (i + 1), :half]
            bi = bu[S5_SEQS * i:S5_SEQS * (i + 1), half:]
            sr, si = ar * sr - ai * si + br, ar * si + ai * sr + bi
            xr_rows.append(sr)
            xi_rows.append(si)
        carry_ref[:, re_cols] = sr
        carry_ref[:, im_cols] = si
        xr = jnp.concatenate(xr_rows, axis=0)
        xi = jnp.concatenate(xi_rows, axis=0)
        ys.append(_bdot(xr, wc_ref[j, :half, :]) + _bdot(xi, wc_ref[j, half:, :]))
    y = jnp.concatenate(ys, axis=1) + u * d_ref[...]
    y = jax.nn.gelu(y)
    y = (y * _sigmoid(_bdot(y, wg_ref[...]))).astype(BF16)
    y = jnp.dot(permt_ref[...], y, preferred_element_type=F32).astype(y_ref.dtype)
    for k in range(S5_SEQS):
        y_ref[k] = y[k * tseg:(k + 1) * tseg]
    hT_ref[...] = carry_ref[...]


def _s5_tables(lam_re, lam_im, log_dt, b_re, b_im, c_re, c_im):
    G, P = lam_re.shape
    H = b_re.shape[2]
    gpb = LANES // H
    nblk = G // gpb
    dt = jnp.exp(log_dt)[:, None]
    mag = jnp.exp(lam_re * dt)
    ab_re = mag * jnp.cos(lam_im * dt)
    ab_im = mag * jnp.sin(lam_im * dt)
    den = lam_re * lam_re + lam_im * lam_im
    z_re = ((ab_re - 1.0) * lam_re + ab_im * lam_im) / den
    z_im = (ab_im * lam_re - (ab_re - 1.0) * lam_im) / den
    bb_re = z_re[..., None] * b_re - z_im[..., None] * b_im
    bb_im = z_re[..., None] * b_im + z_im[..., None] * b_re
    eye = jnp.eye(gpb, dtype=F32)

    def blockdiag_in(bb):
        bb = bb.reshape(nblk, gpb, P, H)
        m = jnp.einsum('jgph,gk->jghkp', bb, eye)
        return m.reshape(nblk, gpb * H, gpb * P)

    def blockdiag_out(cc):
        cc = cc.reshape(nblk, gpb, H, P)
        m = jnp.einsum('jghp,gk->jgpkh', cc, eye)
        return m.reshape(nblk, gpb * P, gpb * H)

    wb = jnp.concatenate([blockdiag_in(bb_re), blockdiag_in(bb_im)], axis=2).astype(BF16)
    wc = jnp.concatenate([blockdiag_out(c_re), blockdiag_out(-c_im)], axis=1).astype(BF16)

    a1r = jnp.broadcast_to(ab_re.reshape(1, G * P), (S5_SEQS, G * P))
    a1i = jnp.broadcast_to(ab_im.reshape(1, G * P), (S5_SEQS, G * P))
    return wb, wc, a1r, a1i


def _s5_state_to_cols(h_re, h_im, nblk):
    B, G, P = h_re.shape
    gpb = G // nblk
    st = jnp.stack([h_re.reshape(B, nblk, gpb * P), h_im.reshape(B, nblk, gpb * P)], axis=2)
    return st.reshape(B, 2 * G * P)


def _s5_cols_to_state(cols, G, P, nblk):
    B = cols.shape[0]
    st = cols.reshape(B, nblk, 2, (G // nblk) * P)
    return st[:, :, 0].reshape(B, G, P), st[:, :, 1].reshape(B, G, P)


def _s5(u, h_re, h_im, lam_re, lam_im, log_dt, b_re, b_im, c_re, c_im, d_skip, w_glu_bf16):
    B, L, W = u.shape
    G, P = lam_re.shape
    assert B % S5_SEQS == 0
    tseg = _time_tile(L, S5_STEPS)
    T = S5_SEQS * tseg
    nblk = W // LANES
    tables = _s5_tables(lam_re, lam_im, log_dt, b_re, b_im, c_re, c_im)
    r = jnp.arange(T)
    perm = ((r % S5_SEQS) * tseg + r // S5_SEQS)[:, None] == jnp.arange(T)[None, :]
    perm = perm.astype(BF16)
    h0 = _s5_state_to_cols(h_re, h_im, nblk)
    ncol = 2 * G * P
    consts = (_tile3(perm), perm.T) + tables + (d_skip.reshape(1, W), w_glu_bf16)
    y, hT = pl.pallas_call(
        functools.partial(_s5_kernel, tseg=tseg),
        out_shape=[jax.ShapeDtypeStruct((B, L, W), BF16), jax.ShapeDtypeStruct((B, ncol), F32)],
        grid=(B // S5_SEQS, L // tseg),
        in_specs=[pl.BlockSpec((S5_SEQS, tseg, W), lambda b, t: (b, t, 0)),
                  pl.BlockSpec((S5_SEQS, ncol), lambda b, t: (b, 0))]
        + [_const_spec(c.shape) for c in consts],
        out_specs=[pl.BlockSpec((S5_SEQS, tseg, W), lambda b, t: (b, t, 0)),
                   pl.BlockSpec((S5_SEQS, ncol), lambda b, t: (b, 0))],
        scratch_shapes=[pltpu.VMEM((S5_SEQS, ncol), F32)],
        compiler_params=_cparams(("parallel", "arbitrary")),
        name="s5_mixer",
    )(u, h0, *consts)
    hr, hi = _s5_cols_to_state(hT, G, P, nblk)
    return y, hr, hi


def _stack_heads(x, lane_lo):
    zero = jnp.zeros_like(x)
    return jnp.concatenate([jnp.where(lane_lo, x, zero), jnp.where(lane_lo, zero, x)], axis=0)


def _head_sum(x, hm2):
    nblk = x.shape[1] // LANES
    R = x.shape[0]
    xs = jnp.concatenate([x[:, j * LANES:(j + 1) * LANES] for j in range(nblk)], axis=0)
    s = _dot2(xs, hm2)
    return jnp.concatenate([s[j * R:(j + 1) * R] for j in range(nblk)], axis=1)


def _rwkv_kernel(p_ref, sp_ref, s0_ref, mu_ref, w0_ref, ww2_ref, a0_ref, wa2_ref, wg2_ref, kk_ref, ka_ref,
                 rk_ref, lng_ref, lnb_ref, hm_ref, tri_ref, y_ref, sh_ref, sT_ref, prev_ref, st_ref, *, width):
    t = pl.program_id(1)
    W = width
    NB, C, _ = p_ref.shape

    @pl.when(t == 0)
    def _():
        prev_ref[...] = sp_ref[...]
        st_ref[...] = s0_ref[...]

    rows = lax.broadcasted_iota(jnp.int32, (C, 1), 0)
    ps = []
    for n in range(NB):
        p = p_ref[n]
        p_prev = jnp.where(rows == 0, prev_ref[n], pltpu.roll(p, 1, axis=0))
        prev_ref[n] = p[C - 1:C, :]
        sh_ref[n] = p[C - 1:C, :]
        ps.append(p + (p_prev - p) * mu_ref[...])
    ps = jnp.concatenate(ps, axis=0)
    r = ps[:, :W]
    k = ps[:, W:2 * W]
    v = ps[:, 2 * W:3 * W]
    wa_low = ps[:, 3 * W:3 * W + LANES]
    g_low = ps[:, 3 * W + LANES:]
    xw = w0_ref[...] + _bdot(jnp.tanh(wa_low), ww2_ref[...])
    lw = -LOG2E * jnp.exp(-_softplus(-xw) - 0.5)
    a = _sigmoid(a0_ref[...] + _bdot(wa_low, wa2_ref[...]))
    g = _bdot(_sigmoid(g_low), wg2_ref[...])
    hm = hm_ref[...]
    kk = k * kk_ref[...]
    kk = kk / jnp.maximum(jnp.sqrt(_head_sum(kk * kk, hm)), 1e-12)
    kmod = k * (1.0 + (a - 1.0) * ka_ref[...])
    bvec = kk * a

    cum = _dot3(tri_ref[...], lw)
    e_in = jnp.exp2(cum)
    e_ex = jnp.exp2(cum - lw)
    e_neg = jnp.exp2(-cum)
    cum_last = [cum[(n + 1) * C - 1:(n + 1) * C, :] for n in range(NB)]
    e_end = jnp.exp2(jnp.concatenate([jnp.broadcast_to(c, (C, W)) for c in cum_last], axis=0) - cum)
    w_end = [jnp.exp2(c) for c in cum_last]
    at = (-kk * e_ex).astype(BF16)
    rt = (r * e_in).astype(BF16)
    bt = (bvec * e_neg).astype(BF16)
    kt = (kmod * e_neg).astype(BF16)
    bw = (bvec * e_end).astype(BF16)
    kw = (kmod * e_end).astype(BF16)
    vb = v.astype(BF16)

    C2 = 2 * C
    si = lax.broadcasted_iota(jnp.int32, (C, C2), 0)
    sj = lax.broadcasted_iota(jnp.int32, (C, C2), 1) % C
    strict = si > sj
    lower = si >= sj
    lane_lo = lax.broadcasted_iota(jnp.int32, (1, LANES), 1) < RWKV_HEAD
    side_lo = lax.broadcasted_iota(jnp.int32, (1, C2), 1) < C
    npair = W // LANES
    chains = [(n, j) for n in range(NB) for j in range(npair)]
    x0, vst, bkw, l_ab, a_ak, a_r = {}, {}, {}, {}, {}, {}
    for c in chains:
        n, j = c
        pair = lambda x: x[n * C:(n + 1) * C, j * LANES:(j + 1) * LANES]
        ar = jnp.concatenate([pair(at), pair(rt)], axis=0)
        bk_st = jnp.concatenate([_stack_heads(pair(bt), lane_lo), _stack_heads(pair(kt), lane_lo)], axis=0)
        m = _bdot_nt(ar, bk_st)
        x0[c] = _bdot_nt(ar, st_ref[n, j])
        vst[c] = _stack_heads(pair(vb), lane_lo)
        bkw[c] = jnp.concatenate([_stack_heads(pair(bw), lane_lo), _stack_heads(pair(kw), lane_lo)], axis=0)
        l_ab[c] = jnp.where(strict, m[:C, :C2], 0.0)
        a_ak[c] = jnp.where(strict, m[:C, C2:], 0.0).astype(BF16)
        a_r[c] = jnp.concatenate([jnp.where(lower, m[C:, :C2], 0.0), jnp.where(lower, m[C:, C2:], 0.0)],
                                 axis=1).astype(BF16)
    def split(x):
        hi = x.astype(BF16)
        return hi, (x - hi.astype(F32)).astype(BF16)

    def lhs3(x):
        hi, lo = split(x)
        return jnp.concatenate([hi, lo, hi], axis=1)

    def rhs3(x, mask):
        hi, lo = split(x)
        hi = _stack_heads(hi, mask)
        return jnp.concatenate([hi, hi, _stack_heads(lo, mask)], axis=0)

    blk_t = si // RWKV_BASE
    blk_s = sj // RWKV_BASE
    diag8 = blk_t == blk_s
    l_q = {c: l_ab[c].astype(BF16) for c in chains}
    zero = jnp.zeros((C, C2), BF16)
    tinv = {c: jnp.where(si == sj, 1.0, 0.0) + jnp.where(diag8, l_q[c], zero).astype(F32) for c in chains}
    pw = {c: jnp.where(diag8, l_q[c], zero) for c in chains}
    for _ in range(int(math.log2(RWKV_BASE)) - 1):
        for c in chains:
            pw[c] = _bdot(pw[c], _stack_heads(pw[c].astype(BF16), side_lo))
        for c in chains:
            tinv[c] = tinv[c] + _bdot(pw[c], _stack_heads(tinv[c].astype(BF16), side_lo))

    def rhs2(x, mask):
        hi, lo = split(x)
        return jnp.concatenate([_stack_heads(hi, mask), _stack_heads(lo, mask)], axis=0)

    b = RWKV_BASE
    while b < C:
        lower_left = ((si // (2 * b)) == (sj // (2 * b))) & (((si // b) % 2) == 1) & (((sj // b) % 2) == 0)
        l_b = {c: jnp.where(lower_left, l_q[c], zero) for c in chains}
        if b == RWKV_BASE:
            lt = {c: _bdot(l_b[c], _stack_heads(tinv[c].astype(BF16), side_lo)) for c in chains}
            for c in chains:
                tinv[c] = tinv[c] + _bdot(tinv[c], _stack_heads(lt[c].astype(BF16), side_lo))
        else:
            lt = {c: jnp.dot(jnp.concatenate([l_b[c], l_b[c]], axis=1), rhs2(tinv[c], side_lo),
                             preferred_element_type=F32) for c in chains}
            for c in chains:
                tinv[c] = tinv[c] + jnp.dot(lhs3(tinv[c]), rhs3(lt[c], side_lo), preferred_element_type=F32)
        b *= 2
    z = {c: x0[c][:C] + jnp.dot(a_ak[c], vst[c], preferred_element_type=F32) for c in chains}
    u = {c: jnp.dot(lhs3(tinv[c]), rhs3(z[c], lane_lo), preferred_element_type=F32) for c in chains}
    uv = {c: jnp.concatenate([_stack_heads(u[c].astype(BF16), lane_lo), vst[c]], axis=0) for c in chains}
    y_p = {c: x0[c][C:] + jnp.dot(a_r[c], uv[c], preferred_element_type=F32) for c in chains}
    for c in chains:
        n, j = c
        st_ref[n, j] = st_ref[n, j] * w_end[n][:, j * LANES:(j + 1) * LANES] + _bdot_tn(uv[c], bkw[c])
    y = jnp.concatenate([jnp.concatenate([y_p[(n, j)] for j in range(npair)], axis=1)
                         for n in range(NB)], axis=0)
    inv_n = 1.0 / RWKV_HEAD
    mean = _head_sum(y, hm) * inv_n
    yc = y - mean
    var = _head_sum(yc * yc, hm) * inv_n
    yn = yc * lax.rsqrt(var + RWKV_GN_EPS) * lng_ref[...] + lnb_ref[...]
    bonus = _head_sum(r * kmod * rk_ref[...], hm) * v
    out = ((yn + bonus) * g).astype(y_ref.dtype)
    for n in range(NB):
        y_ref[n] = out[n * C:(n + 1) * C]

    @pl.when(t == pl.num_programs(1) - 1)
    def _():
        sT_ref[...] = st_ref[...]


def _rwkv(p, shift_prev, s0, mu, w0, w_w2, a0, w_a2, w_g2, k_k, k_a, r_k, lnx_g, lnx_b):
    B, L, cols = p.shape
    H, N = s0.shape[1], s0.shape[2]
    W = H * N
    C = _time_tile(L, 64)
    assert 1 << int(math.log2(C)) == C and 2 * N == LANES
    npair = H // 2
    s0p = s0.reshape(B, npair, 2, N, N)
    z = jnp.zeros_like(s0p[:, :, 0])
    s0bd = jnp.concatenate([jnp.concatenate([s0p[:, :, 0], z], axis=-1),
                            jnp.concatenate([z, s0p[:, :, 1]], axis=-1)], axis=-2)
    lo = w_w2.shape[0]
    ww2p = jnp.concatenate([w_w2, jnp.zeros((LANES - lo, W), F32)], axis=0).astype(BF16)
    wa2p = jnp.concatenate([jnp.zeros((lo, W), F32), w_a2], axis=0).astype(BF16)
    hid = jnp.arange(LANES) // N
    hm = (hid[:, None] == hid[None, :]).astype(BF16)
    hm2 = jnp.concatenate([hm, hm], axis=0)
    nb = RWKV_SEQS if B % RWKV_SEQS == 0 else 1
    tr = jnp.arange(nb * C)
    tri = ((tr[:, None] >= tr[None, :]) & (tr[:, None] // C == tr[None, :] // C)).astype(BF16)
    row = lambda x: x.reshape(1, -1)
    y, sh, sT = pl.pallas_call(
        functools.partial(_rwkv_kernel, width=W),
        out_shape=[jax.ShapeDtypeStruct((B, L, W), BF16),
                   jax.ShapeDtypeStruct((B, 1, cols), F32),
                   jax.ShapeDtypeStruct((B, npair, LANES, LANES), F32)],
        grid=(B // nb, L // C),
        in_specs=[pl.BlockSpec((nb, C, cols), lambda b, t: (b, t, 0)),
                  pl.BlockSpec((nb, 1, cols), lambda b, t: (b, 0, 0)),
                  pl.BlockSpec((nb, npair, LANES, LANES), lambda b, t: (b, 0, 0, 0)),
                  _const_spec((1, cols)), _const_spec((1, W)), _const_spec((LANES, W)), _const_spec((1, W)),
                  _const_spec((LANES, W)), _const_spec((LANES, W)), _const_spec((1, W)), _const_spec((1, W)),
                  _const_spec((1, W)), _const_spec((1, W)), _const_spec((1, W)), _const_spec((2 * LANES, LANES)),
                  _const_spec((nb * C, 3 * nb * C))],
        out_specs=[pl.BlockSpec((nb, C, W), lambda b, t: (b, t, 0)),
                   pl.BlockSpec((nb, 1, cols), lambda b, t: (b, 0, 0)),
                   pl.BlockSpec((nb, npair, LANES, LANES), lambda b, t: (b, 0, 0, 0))],
        scratch_shapes=[pltpu.VMEM((nb, 1, cols), F32), pltpu.VMEM((nb, npair, LANES, LANES), F32)],
        compiler_params=_cparams(("parallel", "arbitrary")),
        name="rwkv7_mixer",
    )(p, shift_prev.reshape(B, 1, cols), s0bd, row(mu), row(w0), ww2p, row(a0), wa2p, w_g2.astype(BF16),
      row(k_k), row(k_a), row(r_k), row(lnx_g), row(lnx_b), hm2, _tile3(tri))
    sT = sT.reshape(B, npair, 2, N, 2, N)
    s_new = jnp.stack([sT[:, :, 0, :, 0, :], sT[:, :, 1, :, 1, :]], axis=2).reshape(B, H, N, N)
    return y, sh[:, 0], s_new


def _hgrn_segment_matrix(C):
    t = jnp.arange(C)[:, None]
    r = jnp.arange(C)[None, :]
    blocks = [r <= t]
    h = C // 2
    while h >= 1:
        mid = (t // (2 * h)) * (2 * h) + h - 1
        is_q = (t & h) != 0
        blocks.append(jnp.where(is_q, (r > mid) & (r <= t), (r > t) & (r <= mid)))
        h //= 2
    return jnp.concatenate(blocks, axis=0).astype(BF16)


def _hgrn_kernel(p_ref, s0_ref, llb_ref, l1lb_ref, omlb_ref, gn_ref, mseg_ref, y_ref, sT_ref, st_ref, *, width):
    t = pl.program_id(1)
    W = width
    H = W // HGRN_HEAD
    NB, C, _ = p_ref.shape

    @pl.when(t == 0)
    def _():
        for n in range(NB):
            for h in range(H):
                st_ref[n, h] = s0_ref[n, h].T

    p = jnp.concatenate([p_ref[n] for n in range(NB)], axis=0)
    q = _silu(p[:, :W])
    f = p[:, W:2 * W]
    v = p[:, 2 * W:3 * W]
    g = p[:, 3 * W:]
    ef = jnp.exp(-jnp.abs(f))
    lsig = jnp.minimum(f, 0.0) - jnp.log(1.0 + ef)
    c1 = llb_ref[...]
    c2 = l1lb_ref[...] + lsig
    log_f = jnp.maximum(c1, c2) + jnp.log(1.0 + jnp.exp(-jnp.abs(c1 - c2)))
    kv = omlb_ref[...] * (jnp.where(f >= 0.0, ef, 1.0) / (1.0 + ef))
    log2_f = log_f * LOG2E
    d_all = [_dot3(mseg_ref[...], log2_f[n * C:(n + 1) * C]) for n in range(NB)]
    b = jnp.concatenate([d[:C] for d in d_all], axis=0)
    b_last = [d[C - 1:C, :] for d in d_all]
    q_b = q.astype(BF16)
    kv_b = kv.astype(BF16)
    qe = jnp.exp2(b).astype(BF16) * q_b
    ke = jnp.exp2(jnp.concatenate([jnp.broadcast_to(bl, (C, W)) for bl in b_last], axis=0) - b).astype(BF16) * kv_b
    e_last = [jnp.exp2(bl) for bl in b_last]
    qk = q * kv
    vb = v.astype(BF16)
    rowi = lax.broadcasted_iota(jnp.int32, (NB * C, 1), 0)
    ti = lax.broadcasted_iota(jnp.int32, (C, C), 0)
    tj = lax.broadcasted_iota(jnp.int32, (C, C), 1)
    zs, masks = [], []
    hsz = C // 2
    lvl = 1
    while hsz >= 1:
        is_q = (rowi & hsz) != 0
        seg = jnp.concatenate([d[lvl * C:(lvl + 1) * C] for d in d_all], axis=0)
        zs.append(jnp.exp2(seg).astype(BF16) * jnp.where(is_q, q_b, kv_b))
        masks.append(((ti & hsz) != 0) & ((tj & hsz) == 0) & ((ti // (2 * hsz)) == (tj // (2 * hsz))))
        hsz //= 2
        lvl += 1
    chains = [(n, h) for n in range(NB) for h in range(H)]
    blk = lambda x, c: x[c[0] * C:(c[0] + 1) * C, c[1] * HGRN_HEAD:(c[1] + 1) * HGRN_HEAD]
    att = {}
    for c in chains:
        for z, mask in zip(zs, masks):
            zh = blk(z, c)
            term = jnp.where(mask, _bdot_nt(zh, zh), 0.0)
            att[c] = term if c not in att else att[c] + term
    o = {c: (_bdot(att[c], blk(vb, c)) + _bdot_nt(blk(qe, c), st_ref[c[0], c[1]])
             + jnp.sum(blk(qk, c), axis=-1, keepdims=True) * blk(v, c)) for c in chains}
    for c in chains:
        n, h = c
        st_ref[n, h] = (st_ref[n, h] * e_last[n][:, h * HGRN_HEAD:(h + 1) * HGRN_HEAD]
                        + _bdot_tn(blk(vb, c), blk(ke, c)))
    for c in chains:
        o[c] = o[c] * lax.rsqrt(jnp.mean(o[c] * o[c], axis=-1, keepdims=True) + NORM_EPS) * gn_ref[...]
    o = jnp.concatenate([jnp.concatenate([o[(n, h)] for h in range(H)], axis=1) for n in range(NB)], axis=0)
    out = (o * _silu(g)).astype(y_ref.dtype)
    for n in range(NB):
        y_ref[n] = out[n * C:(n + 1) * C]

    @pl.when(t == pl.num_programs(1) - 1)
    def _():
        for n in range(NB):
            for h in range(H):
                sT_ref[n, h] = st_ref[n, h].T


def _hgrn(p, s0, lb, gnorm):
    B, L, cols = p.shape
    W = cols // 4
    H = W // HGRN_HEAD
    C = _time_tile(L, 64)
    assert 1 << int(math.log2(C)) == C
    mseg = _tile3(_hgrn_segment_matrix(C))
    nb = HGRN_SEQS if B % HGRN_SEQS == 0 else 1
    row = lambda x: x.reshape(1, -1)
    y, sT = pl.pallas_call(
        functools.partial(_hgrn_kernel, width=W),
        out_shape=[jax.ShapeDtypeStruct((B, L, W), BF16),
                   jax.ShapeDtypeStruct((B, H, HGRN_HEAD, HGRN_HEAD), F32)],
        grid=(B // nb, L // C),
        in_specs=[pl.BlockSpec((nb, C, cols), lambda b, t: (b, t, 0)),
                  pl.BlockSpec((nb, H, HGRN_HEAD, HGRN_HEAD), lambda b, t: (b, 0, 0, 0)),
                  _const_spec((1, W)), _const_spec((1, W)), _const_spec((1, W)), _const_spec((1, HGRN_HEAD)),
                  _const_spec(mseg.shape)],
        out_specs=[pl.BlockSpec((nb, C, W), lambda b, t: (b, t, 0)),
                   pl.BlockSpec((nb, H, HGRN_HEAD, HGRN_HEAD), lambda b, t: (b, 0, 0, 0))],
        scratch_shapes=[pltpu.VMEM((nb, H, HGRN_HEAD, HGRN_HEAD), F32)],
        compiler_params=_cparams(("parallel", "arbitrary")),
        name="hgrn2_mixer",
    )(p, s0, row(jnp.log(lb)), row(jnp.log1p(-lb)), row(1.0 - lb), row(gnorm), mseg)
    return y, sT


def _trunk(x, mods_all, s5_re, s5_im, rwkv_s, rwkv_shift, hgrn_s, P, Pb):
    depth = P['norm_mix'].shape[0]
    lb_all = jax.nn.softmax(P['hgrn_lower_bounds'], axis=0)
    lb_all = jnp.cumsum(lb_all, axis=0) - lb_all[0]
    s5_width = P['s5_w_glu'].shape[1]
    n_re, n_im, n_rw, n_sh, n_hg = [], [], [], [], []
    B, L, D = x.shape
    flat = L < FLAT_BELOW
    if flat:
        x = x.reshape(1, B * L, D)
        mods_all = mods_all.transpose(0, 2, 1, 3)[:, None]
    else:
        mods_all = mods_all[:, :, :, None, :]
    seq = lambda a: a.reshape(B, L, a.shape[-1])
    tok = lambda a: a.reshape(x.shape[0], x.shape[1], a.shape[-1])
    for l in range(depth):
        mods = mods_all[l]
        x = _ffn(x, mods, P['norm_ffn1'][l], Pb['ffn1_w1'], Pb['ffn1_w3'], Pb['ffn1_w2'], l, 0)
        i = l // 2
        if l % 2 == 0:
            n_in = Pb['ab_w_in'].shape[2]
            u, pr = _inproj(x, mods, P['norm_mix'][l], Pb['ab_w_in'][i], (s5_width, n_in - s5_width))
            ya, hr, hi = _s5(seq(u), s5_re[i], s5_im[i], P['s5_lam_re'][i], P['s5_lam_im'][i],
                             P['s5_log_dt'][i], P['s5_b_re'][i], P['s5_b_im'][i], P['s5_c_re'][i],
                             P['s5_c_im'][i], P['s5_d'][i], Pb['s5_w_glu'][i])
            yb, sh_new, rw_new = _rwkv(seq(pr), rwkv_shift[i], rwkv_s[i], P['rwkv_mu'][i], P['rwkv_w0'][i],
                                       P['rwkv_w_w2'][i], P['rwkv_a0'][i], P['rwkv_w_a2'][i],
                                       P['rwkv_w_g2'][i], P['rwkv_k_k'][i], P['rwkv_k_a'][i],
                                       P['rwkv_r_k'][i], P['rwkv_lnx_g'][i], P['rwkv_lnx_b'][i])
            ys, w_out = (tok(ya), tok(yb)), Pb['ab_w_out'][i]
            n_re.append(hr)
            n_im.append(hi)
            n_rw.append(rw_new)
            n_sh.append(sh_new)
        else:
            (pc,) = _inproj(x, mods, P['norm_mix'][l], Pb['c_w_in'][i], (Pb['c_w_in'].shape[2],))
            yc, hg_new = _hgrn(seq(pc), hgrn_s[i], lb_all[l], P['hgrn_gnorm'][i])
            ys, w_out = (tok(yc),), Pb['c_w_out'][i]
            n_hg.append(hg_new)
        fin = P['final_norm'] if l == depth - 1 else None
        x = _ffn(x, mods, P['norm_ffn2'][l], Pb['ffn2_w1'], Pb['ffn2_w3'], Pb['ffn2_w2'], l, 2,
                 final_gain=fin, ys=ys, w_out=w_out)
    return (x.reshape(B, L, D), jnp.stack(n_re), jnp.stack(n_im), jnp.stack(n_rw), jnp.stack(n_sh),
            jnp.stack(n_hg))


def kernel(x_prompt, x_sample, state_s5_re, state_s5_im, state_rwkv, state_rwkv_shift, state_hgrn, c_prompt, c_sample, w_ada, b_ada, norm_ffn1, norm_mix, norm_ffn2, ffn1_w1, ffn1_w3, ffn1_w2, ffn2_w1, ffn2_w3, ffn2_w2, ab_w_in, ab_w_out, s5_lam_re, s5_lam_im, s5_log_dt, s5_b_re, s5_b_im, s5_c_re, s5_c_im, s5_d, s5_w_glu, rwkv_mu, rwkv_w0, rwkv_w_w2, rwkv_a0, rwkv_w_a2, rwkv_w_g2, rwkv_k_k, rwkv_k_a, rwkv_r_k, rwkv_lnx_g, rwkv_lnx_b, c_w_in, c_w_out, hgrn_lower_bounds, hgrn_gnorm, final_norm):
    P = dict(norm_ffn1=norm_ffn1, norm_mix=norm_mix, norm_ffn2=norm_ffn2,
             s5_lam_re=s5_lam_re, s5_lam_im=s5_lam_im, s5_log_dt=s5_log_dt, s5_b_re=s5_b_re, s5_b_im=s5_b_im,
             s5_c_re=s5_c_re, s5_c_im=s5_c_im, s5_d=s5_d, s5_w_glu=s5_w_glu, rwkv_mu=rwkv_mu,
             rwkv_w0=rwkv_w0, rwkv_w_w2=rwkv_w_w2, rwkv_a0=rwkv_a0, rwkv_w_a2=rwkv_w_a2, rwkv_w_g2=rwkv_w_g2,
             rwkv_k_k=rwkv_k_k, rwkv_k_a=rwkv_k_a, rwkv_r_k=rwkv_r_k, rwkv_lnx_g=rwkv_lnx_g,
             rwkv_lnx_b=rwkv_lnx_b, hgrn_lower_bounds=hgrn_lower_bounds, hgrn_gnorm=hgrn_gnorm,
             final_norm=final_norm)
    Pb = {n: w.astype(BF16) for n, w in dict(
        ffn1_w1=ffn1_w1, ffn1_w3=ffn1_w3, ffn1_w2=ffn1_w2, ffn2_w1=ffn2_w1, ffn2_w3=ffn2_w3, ffn2_w2=ffn2_w2,
        ab_w_in=ab_w_in, ab_w_out=ab_w_out, c_w_in=c_w_in, c_w_out=c_w_out, s5_w_glu=s5_w_glu).items()}
    depth, d = norm_mix.shape
    bp, bs = x_prompt.shape[0], x_sample.shape[0]
    mods = _ada(jnp.concatenate([c_prompt, c_sample], axis=0), w_ada, b_ada)
    mods = mods.reshape(depth, bp + bs, N_MOD, d)
    dt = x_prompt.dtype
    zeros_like_b = lambda s: jnp.zeros((s.shape[0], bp) + s.shape[2:], dt)
    outs_p = _trunk(x_prompt, mods[:, :bp], zeros_like_b(state_s5_re), zeros_like_b(state_s5_im),
                    zeros_like_b(state_rwkv), zeros_like_b(state_rwkv_shift), zeros_like_b(state_hgrn), P, Pb)
    outs_s = _trunk(x_sample, mods[:, bp:], state_s5_re, state_s5_im, state_rwkv, state_rwkv_shift,
                    state_hgrn, P, Pb)
    return (outs_p[0], outs_s[0]) + tuple(outs_p[1:]) + tuple(outs_s[1:])
```
